```python
import math
import jax, jax.numpy as jnp
from jax import lax
import numpy as np

D_MODEL = 1024
BATCH = 8
SEQ = 2048
DEPTH = 2

N_MIXERS = 2
N_SSD_LAYERS = (DEPTH + 1) // 2
N_CONV_LAYERS = DEPTH // 2
NORM_EPS = 1e-5
ADA_MODS = 6

SSD_EXPAND = 2
SSD_D_INNER = SSD_EXPAND * D_MODEL
SSD_HEAD_DIM = 64
SSD_N_HEADS = SSD_D_INNER // SSD_HEAD_DIM
SSD_N_GROUPS = 4
SSD_HEADS_PER_GROUP = SSD_N_HEADS // SSD_N_GROUPS
SSD_D_STATE = 128
SSD_CONV_K = 4
SSD_CHUNK = 128
SSD_CONV_DIM = SSD_D_INNER + 2 * SSD_N_GROUPS * SSD_D_STATE
SSD_IN_DIM = SSD_D_INNER + SSD_CONV_DIM + SSD_N_HEADS
DT_MIN = 1e-3
DT_MAX = 1e-1

SC_WIDTH = D_MODEL
SC_CONV_K = 3

D_FF = 4 * D_MODEL

kernel_name = "hybrid_ssd_shortconv_adaln_trunk"


def rmsnorm(x, g, eps=NORM_EPS):
    xf = x.astype(jnp.float32)
    y = xf * lax.rsqrt(jnp.mean(xf * xf, axis=-1, keepdims=True) + eps)
    return (y * g.astype(jnp.float32)).astype(x.dtype)


def causal_dwconv(x, w, b=None):
    k, ch = w.shape
    out = lax.conv_general_dilated(
        x, w[:, None, :].astype(x.dtype), window_strides=(1,), padding=[(k - 1, 0)],
        dimension_numbers=("NWC", "WIO", "NWC"), feature_group_count=ch)
    if b is not None:
        out = out + b.astype(x.dtype)
    return out


def ssd_chunked(xs, dt, A, Bm, Cm):
    b, L, g, r, p = xs.shape
    n = Bm.shape[-1]
    nc = L // SSD_CHUNK
    xs = xs.astype(jnp.float32).reshape(b, nc, SSD_CHUNK, g, r, p)
    dt = dt.reshape(b, nc, SSD_CHUNK, g, r)
    Bc = Bm.astype(jnp.float32).reshape(b, nc, SSD_CHUNK, g, n)
    Cc = Cm.astype(jnp.float32).reshape(b, nc, SSD_CHUNK, g, n)
    X = xs * dt[..., None]
    Acs = jnp.cumsum(dt * A, axis=2)

    causal = jnp.tril(jnp.ones((SSD_CHUNK, SSD_CHUNK), dtype=bool))[:, :, None, None]
    seg = Acs[:, :, :, None] - Acs[:, :, None, :]
    Lmat = jnp.exp(jnp.where(causal, seg, -jnp.inf))
    scores = jnp.einsum("bclgn,bcsgn->bclsg", Cc, Bc)
    M = scores[..., None] * Lmat
    y_diag = jnp.einsum("bclsgr,bcsgrp->bclgrp", M, X)

    decay_states = jnp.exp(Acs[:, :, -1:] - Acs)
    states = jnp.einsum("bcsgn,bcsgrp->bcgrpn", Bc, X * decay_states[..., None])

    chunk_decay = jnp.exp(Acs[:, :, -1])

    def step(carry, inp):
        st, dec = inp
        return carry * dec[..., None, None] + st, carry

    init = jnp.zeros((b, g, r, p, n), dtype=states.dtype)
    _, prev = lax.scan(step, init, (jnp.moveaxis(states, 1, 0), jnp.moveaxis(chunk_decay, 1, 0)))
    prev = jnp.moveaxis(prev, 0, 1)

    y_off = jnp.einsum("bclgn,bcgrpn->bclgrp", Cc, prev) * jnp.exp(Acs)[..., None]
    return (y_diag + y_off).reshape(b, L, g, r, p)


def ssd_mixer(h, in_w, conv_w, conv_b, dt_bias, A_log, D_skip, norm_w, out_w):
    b, L, _ = h.shape
    G, R, P, N = SSD_N_GROUPS, SSD_HEADS_PER_GROUP, SSD_HEAD_DIM, SSD_D_STATE
    zxbcdt = h @ in_w
    z = zxbcdt[..., :SSD_D_INNER]
    xBC = zxbcdt[..., SSD_D_INNER:SSD_D_INNER + SSD_CONV_DIM]
    dt_raw = zxbcdt[..., SSD_D_INNER + SSD_CONV_DIM:]
    xBC = jax.nn.silu(causal_dwconv(xBC, conv_w, conv_b))
    xs = xBC[..., :SSD_D_INNER].reshape(b, L, G, R, P)
    Bm = xBC[..., SSD_D_INNER:SSD_D_INNER + G * N].reshape(b, L, G, N)
    Cm = xBC[..., SSD_D_INNER + G * N:].reshape(b, L, G, N)
    dt = jax.nn.softplus(dt_raw.astype(jnp.float32) + dt_bias.astype(jnp.float32)).reshape(b, L, G, R)
    A = -jnp.exp(A_log.astype(jnp.float32)).reshape(G, R)
    y = ssd_chunked(xs, dt, A, Bm, Cm)
    y = y + D_skip.astype(jnp.float32).reshape(G, R, 1) * xs.astype(jnp.float32)
    yg = y.reshape(b, L, SSD_D_INNER) * jax.nn.silu(z.astype(jnp.float32))
    yg = yg.reshape(b, L, G, SSD_D_INNER // G)
    yg = yg * lax.rsqrt(jnp.mean(yg * yg, axis=-1, keepdims=True) + NORM_EPS)
    yg = yg.reshape(b, L, SSD_D_INNER) * norm_w.astype(jnp.float32)
    return yg.astype(h.dtype) @ out_w


def short_conv_mixer(h, in_w, conv_w, out_w):
    proj = h @ in_w
    Bg, Cg, xv = jnp.split(proj, 3, axis=-1)
    y = Bg * causal_dwconv(Cg * xv, conv_w)
    return y @ out_w


def sqrelu_mlp(h, up_w, down_w):
    a = jax.nn.relu(h @ up_w)
    return (a * a) @ down_w


def setup_inputs(seed: int = 0) -> dict:
    key = jax.random.key(seed)
    ks = jax.random.split(key, 24)
    f32 = jnp.float32
    D = D_MODEL
    nrm = lambda k, shape, s: jax.random.normal(k, shape, f32) * s
    x = jax.random.normal(ks[0], (BATCH, SEQ, D), f32)
    c = jax.random.normal(ks[1], (BATCH, D), f32)
    ada_w = nrm(ks[2], (DEPTH, D, ADA_MODS * D), 0.5 * D ** -0.5)
    ada_b = nrm(ks[3], (DEPTH, ADA_MODS * D), 0.02)
    mix_norm_w = 1.0 + nrm(ks[4], (DEPTH, D), 0.02)
    mlp_norm_w = 1.0 + nrm(ks[5], (DEPTH, D), 0.02)
    mlp_up = nrm(ks[6], (DEPTH, D, D_FF), D ** -0.5)
    mlp_down = nrm(ks[7], (DEPTH, D_FF, D), D_FF ** -0.5)
    ssd_in_w = nrm(ks[8], (N_SSD_LAYERS, D, SSD_IN_DIM), D ** -0.5)
    ssd_conv_w = nrm(ks[9], (N_SSD_LAYERS, SSD_CONV_K, SSD_CONV_DIM), SSD_CONV_K ** -0.5)
    ssd_conv_b = nrm(ks[10], (N_SSD_LAYERS, SSD_CONV_DIM), 0.02)
    u = jax.random.uniform(ks[11], (N_SSD_LAYERS, SSD_N_HEADS), f32)
    dt0 = jnp.exp(u * (math.log(DT_MAX) - math.log(DT_MIN)) + math.log(DT_MIN))
    ssd_dt_bias = dt0 + jnp.log(-jnp.expm1(-dt0))
    ssd_A_log = jnp.log(jax.random.uniform(ks[12], (N_SSD_LAYERS, SSD_N_HEADS), f32, 1.0, 16.0))
    ssd_D = 1.0 + nrm(ks[13], (N_SSD_LAYERS, SSD_N_HEADS), 0.02)
    ssd_norm_w = 1.0 + nrm(ks[14], (N_SSD_LAYERS, SSD_D_INNER), 0.02)
    ssd_out_w = nrm(ks[15], (N_SSD_LAYERS, SSD_D_INNER, D), SSD_D_INNER ** -0.5)
    sc_in_w = nrm(ks[16], (N_CONV_LAYERS, D, 3 * SC_WIDTH), D ** -0.5)
    sc_conv_w = nrm(ks[17], (N_CONV_LAYERS, SC_CONV_K, SC_WIDTH), SC_CONV_K ** -0.5)
    sc_out_w = nrm(ks[18], (N_CONV_LAYERS, SC_WIDTH, D), SC_WIDTH ** -0.5)
    final_norm_w = 1.0 + nrm(ks[19], (D,), 0.02)
    return {"x": x, "c": c, "ada_w": ada_w, "ada_b": ada_b,
            "mix_norm_w": mix_norm_w, "mlp_norm_w": mlp_norm_w,
            "mlp_up": mlp_up, "mlp_down": mlp_down,
            "ssd_in_w": ssd_in_w, "ssd_conv_w": ssd_conv_w, "ssd_conv_b": ssd_conv_b,
            "ssd_dt_bias": ssd_dt_bias, "ssd_A_log": ssd_A_log, "ssd_D": ssd_D,
            "ssd_norm_w": ssd_norm_w, "ssd_out_w": ssd_out_w,
            "sc_in_w": sc_in_w, "sc_conv_w": sc_conv_w, "sc_out_w": sc_out_w,
            "final_norm_w": final_norm_w}


def reference(x, c, ada_w, ada_b, mix_norm_w, mlp_norm_w, mlp_up, mlp_down,
              ssd_in_w, ssd_conv_w, ssd_conv_b, ssd_dt_bias, ssd_A_log, ssd_D,
              ssd_norm_w, ssd_out_w, sc_in_w, sc_conv_w, sc_out_w, final_norm_w):
    cond = jax.nn.silu(c.astype(x.dtype))
    for i in range(DEPTH):
        mod = cond @ ada_w[i] + ada_b[i]
        sh_m, sc_m, g_m, sh_f, sc_f, g_f = [m[:, None, :] for m in jnp.split(mod, ADA_MODS, axis=-1)]
        h = rmsnorm(x, mix_norm_w[i]) * (1.0 + sc_m) + sh_m
        j = i // N_MIXERS
        if i % N_MIXERS == 0:
            y = ssd_mixer(h, ssd_in_w[j], ssd_conv_w[j], ssd_conv_b[j], ssd_dt_bias[j],
                          ssd_A_log[j], ssd_D[j], ssd_norm_w[j], ssd_out_w[j])
        else:
            y = short_conv_mixer(h, sc_in_w[j], sc_conv_w[j], sc_out_w[j])
        x = x + g_m * y
        h = rmsnorm(x, mlp_norm_w[i]) * (1.0 + sc_f) + sh_f
        x = x + g_f * sqrelu_mlp(h, mlp_up[i], mlp_down[i])
    return rmsnorm(x, final_norm_w)
```

```python
import functools

import jax
import jax.numpy as jnp
from jax import lax
from jax.experimental import pallas as pl
from jax.experimental.pallas import tpu as pltpu

F32 = jnp.float32
BF16 = jnp.bfloat16

D_MODEL = 1024
BATCH = 8
SEQ = 2048
TOKENS = BATCH * SEQ
DEPTH = 2
NORM_EPS = 1e-5
ADA_MODS = 6

SSD_D_INNER = 2048
SSD_HEAD_DIM = 64
SSD_N_HEADS = 32
SSD_N_GROUPS = 4
SSD_D_STATE = 128
SSD_CONV_K = 4
SSD_CHUNK = 128
SSD_BC = SSD_N_GROUPS * SSD_D_STATE
SSD_CONV_DIM = SSD_D_INNER + 2 * SSD_BC
SSD_ZX_DIM = SSD_D_INNER + SSD_CONV_DIM
GROUP_WIDTH = SSD_D_INNER // SSD_N_GROUPS

SC_WIDTH = D_MODEL
SC_CONV_K = 3
D_FF = 4 * D_MODEL

LANES = 128
SUBLANES = 8
VMEM_LIMIT = 48 * 1024 * 1024


def _silu(v):
    return v * jax.nn.sigmoid(v)


def _norm_mod(x, g, sc, sh):
    ms = jnp.mean(x * x, axis=-1, keepdims=True)
    y = x * lax.rsqrt(ms + NORM_EPS)
    return (y * g) * (1.0 + sc) + sh


def _shift_rows(cur, prev8, s):
    rolled = pltpu.roll(cur, s, 0)
    prev_rolled = pltpu.roll(prev8, s, 0)
    row = lax.broadcasted_iota(jnp.int32, prev8.shape, 0)
    first = jnp.where(row < s, prev_rolled, rolled[0:SUBLANES])
    return jnp.concatenate([first, rolled[SUBLANES:]], axis=0)


def _ada_kernel(c_ref, w_ref, b_ref, o_ref):
    cond = _silu(c_ref[...])
    o_ref[0] = jnp.dot(cond.astype(BF16), w_ref[0].astype(BF16),
                       preferred_element_type=F32) + b_ref[0]


def _ada_mod(c, ada_w, ada_b):
    tn = 1536
    n = ADA_MODS * D_MODEL
    return pl.pallas_call(
        _ada_kernel,
        grid=(DEPTH, n // tn),
        in_specs=[
            pl.BlockSpec((BATCH, D_MODEL), lambda i, j: (0, 0)),
            pl.BlockSpec((1, D_MODEL, tn), lambda i, j: (i, 0, j)),
            pl.BlockSpec((1, 1, tn), lambda i, j: (i, 0, j)),
        ],
        out_specs=pl.BlockSpec((1, BATCH, tn), lambda i, j: (i, 0, j)),
        out_shape=jax.ShapeDtypeStruct((DEPTH, BATCH, n), F32),
        compiler_params=pltpu.CompilerParams(
            dimension_semantics=("parallel", "parallel"), vmem_limit_bytes=VMEM_LIMIT),
        name="ada_mod",
    )(c, ada_w, ada_b.reshape(DEPTH, 1, n))


def _in_proj_kernel(x_ref, g_ref, sc_ref, sh_ref, w_ref, o_ref, h_ref):
    @pl.when(pl.program_id(1) == 0)
    def _():
        h_ref[...] = _norm_mod(x_ref[...], g_ref[...], sc_ref[0], sh_ref[0]).astype(BF16)

    o_ref[...] = jnp.dot(h_ref[...], w_ref[...], preferred_element_type=F32)


def _in_proj_dt_kernel(x_ref, g_ref, sc_ref, sh_ref, w_ref, wdt_ref, dtb_ref, o_ref, dt_ref, h_ref):
    @pl.when(pl.program_id(1) == 0)
    def _():
        h = _norm_mod(x_ref[...], g_ref[...], sc_ref[0], sh_ref[0]).astype(BF16)
        h_ref[...] = h
        raw = jnp.dot(h, wdt_ref[...], preferred_element_type=F32) + dtb_ref[...]
        dt_ref[...] = jnp.maximum(raw, 0.0) + jnp.log1p(jnp.exp(-jnp.abs(raw)))

    o_ref[...] = jnp.dot(h_ref[...], w_ref[...], preferred_element_type=F32)


def _in_proj(x, g, sc, sh, w, tm, tn, wdt=None, dtb=None):
    n = w.shape[1]
    per_batch = SEQ // tm
    in_specs = [
        pl.BlockSpec((tm, D_MODEL), lambda i, j: (i, 0)),
        pl.BlockSpec((1, D_MODEL), lambda i, j: (0, 0)),
        pl.BlockSpec((1, 1, D_MODEL), lambda i, j: (i // per_batch, 0, 0)),
        pl.BlockSpec((1, 1, D_MODEL), lambda i, j: (i // per_batch, 0, 0)),
        pl.BlockSpec((D_MODEL, tn), lambda i, j: (0, j)),
    ]
    out_specs = pl.BlockSpec((tm, tn), lambda i, j: (i, j))
    out_shape = jax.ShapeDtypeStruct((TOKENS, n), F32)
    args = [x, g, sc, sh, w]
    body = _in_proj_kernel
    if wdt is not None:
        in_specs += [pl.BlockSpec((D_MODEL, LANES), lambda i, j: (0, 0)),
                     pl.BlockSpec((1, LANES), lambda i, j: (0, 0))]
        out_specs = [out_specs, pl.BlockSpec((tm, LANES), lambda i, j: (i, 0))]
        out_shape = [out_shape, jax.ShapeDtypeStruct((TOKENS, LANES), F32)]
        args += [wdt, dtb]
        body = _in_proj_dt_kernel
    return pl.pallas_call(
        body,
        grid=(TOKENS // tm, n // tn),
        in_specs=in_specs,
        out_specs=out_specs,
        out_shape=out_shape,
        scratch_shapes=[pltpu.VMEM((tm, D_MODEL), BF16)],
        compiler_params=pltpu.CompilerParams(
            dimension_semantics=("parallel", "arbitrary"), vmem_limit_bytes=VMEM_LIMIT),
        name="in_proj_dt" if wdt is not None else "in_proj",
    )(*args)


def _ssd_kernel(zx_ref, prev_ref, dt_ref, cw_ref, cb_ref, alog_ref, dskip_ref, nw_ref,
                sel_ref, tri_ref, o_ref, state_ref):
    chunk = pl.program_id(1)

    @pl.when(chunk == 0)
    def _():
        state_ref[...] = jnp.zeros_like(state_ref)

    cur = zx_ref[:, SSD_D_INNER:]
    prev8 = jnp.where(chunk == 0, 0.0, prev_ref[:, SSD_D_INNER:])
    acc = cur * cw_ref[SSD_CONV_K - 1:SSD_CONV_K, :] + cb_ref[...]
    for s in range(1, SSD_CONV_K):
        k = SSD_CONV_K - 1 - s
        acc = acc + _shift_rows(cur, prev8, s) * cw_ref[k:k + 1, :]
    xbc = _silu(acc)
    xs = xbc[:, :SSD_D_INNER]
    bm = xbc[:, SSD_D_INNER:SSD_D_INNER + SSD_BC]
    cm = xbc[:, SSD_D_INNER + SSD_BC:]

    dt = dt_ref[...]
    a = dt * (-jnp.exp(alog_ref[...]))
    acs = jnp.dot(tri_ref[...], a, precision=lax.Precision.HIGHEST, preferred_element_type=F32)
    acs_t = acs.T
    expanded = jnp.dot(jnp.concatenate([dt, acs], axis=0), sel_ref[...],
                       precision=lax.Precision.HIGHEST, preferred_element_type=F32)
    dt_e = expanded[:SSD_CHUNK]
    acs_e = expanded[SSD_CHUNK:]
    last_e = acs_e[SSD_CHUNK - 1:SSD_CHUNK, :]
    out_scale = jnp.exp(acs_e)
    chunk_decay = out_scale[SSD_CHUNK - 1:SSD_CHUNK, :]
    x_dt = xs * dt_e
    x_state = (x_dt * jnp.exp(last_e - acs_e)).astype(BF16)

    li = lax.broadcasted_iota(jnp.int32, (SSD_CHUNK, SSD_CHUNK), 0)
    si = lax.broadcasted_iota(jnp.int32, (SSD_CHUNK, SSD_CHUNK), 1)
    causal = li >= si
    left_half = si < SSD_HEAD_DIM

    for g in range(SSD_N_GROUPS):
        gcol = g * GROUP_WIDTH
        b_g = bm[:, g * SSD_D_STATE:(g + 1) * SSD_D_STATE]
        c_g = cm[:, g * SSD_D_STATE:(g + 1) * SSD_D_STATE].astype(BF16)
        scores = lax.dot_general(c_g, b_g.astype(BF16), (((1,), (1,)), ((), ())),
                                 preferred_element_type=F32)
        state = state_ref[g]
        y_off = jnp.dot(c_g, state.astype(BF16), preferred_element_type=F32)
        y_off = y_off * out_scale[:, gcol:gcol + GROUP_WIDTH]
        new_state = jnp.dot(b_g.T.astype(BF16), x_state[:, gcol:gcol + GROUP_WIDTH],
                            preferred_element_type=F32)
        state_ref[g] = state * chunk_decay[:, gcol:gcol + GROUP_WIDTH] + new_state

        y_parts = []
        for pair in range(GROUP_WIDTH // LANES):
            col = gcol + pair * LANES
            h0 = col // SSD_HEAD_DIM
            m = []
            for h in (h0, h0 + 1):
                seg = acs[:, h:h + 1] - acs_t[h:h + 1, :]
                decay = jnp.where(causal, jnp.exp(seg), 0.0)
                m.append((scores * decay).astype(BF16))
            lhs = jnp.concatenate(m, axis=1)
            xp = x_dt[:, col:col + LANES]
            rhs = jnp.concatenate([jnp.where(left_half, xp, 0.0),
                                   jnp.where(left_half, 0.0, xp)], axis=0).astype(BF16)
            y_diag = jnp.dot(lhs, rhs, preferred_element_type=F32)
            y_parts.append(y_diag + y_off[:, pair * LANES:(pair + 1) * LANES]
                           + dskip_ref[:, col:col + LANES] * xs[:, col:col + LANES])
        y = jnp.concatenate(y_parts, axis=1)

        yg = y * _silu(zx_ref[:, gcol:gcol + GROUP_WIDTH])
        ms = jnp.mean(yg * yg, axis=-1, keepdims=True)
        yn = yg * lax.rsqrt(ms + NORM_EPS) * nw_ref[:, gcol:gcol + GROUP_WIDTH]
        o_ref[:, gcol:gcol + GROUP_WIDTH] = yn.astype(BF16)


def _ssd_core(zx, dt, conv_w, conv_b, a_log, d_skip, norm_w):
    n_chunks = SEQ // SSD_CHUNK
    blocks8 = SSD_CHUNK // SUBLANES
    head_of_lane = jnp.arange(SSD_D_INNER) // SSD_HEAD_DIM
    sel = (jnp.arange(LANES)[:, None] == head_of_lane[None, :]).astype(F32)
    tri = (jnp.arange(SSD_CHUNK)[:, None] >= jnp.arange(SSD_CHUNK)[None, :]).astype(F32)
    a_log_pad = jnp.zeros((1, LANES), F32).at[0, :SSD_N_HEADS].set(a_log)
    d_e = jnp.repeat(d_skip, SSD_HEAD_DIM).reshape(1, SSD_D_INNER)

    def const(shape):
        return pl.BlockSpec(shape, lambda b, c: (0, 0))

    return pl.pallas_call(
        _ssd_kernel,
        grid=(BATCH, n_chunks),
        in_specs=[
            pl.BlockSpec((SSD_CHUNK, SSD_ZX_DIM), lambda b, c: (b * n_chunks + c, 0)),
            pl.BlockSpec((SUBLANES, SSD_ZX_DIM),
                         lambda b, c: (jnp.maximum((b * n_chunks + c) * blocks8 - 1, 0), 0)),
            pl.BlockSpec((SSD_CHUNK, LANES), lambda b, c: (b * n_chunks + c, 0)),
            const((SSD_CONV_K, SSD_CONV_DIM)),
            const((1, SSD_CONV_DIM)),
            const((1, LANES)),
            const((1, SSD_D_INNER)),
            const((1, SSD_D_INNER)),
            const((LANES, SSD_D_INNER)),
            const((SSD_CHUNK, SSD_CHUNK)),
        ],
        out_specs=pl.BlockSpec((SSD_CHUNK, SSD_D_INNER), lambda b, c: (b * n_chunks + c, 0)),
        out_shape=jax.ShapeDtypeStruct((TOKENS, SSD_D_INNER), BF16),
        scratch_shapes=[pltpu.VMEM((SSD_N_GROUPS, SSD_D_STATE, GROUP_WIDTH), F32)],
        compiler_params=pltpu.CompilerParams(
            dimension_semantics=("parallel", "arbitrary"), vmem_limit_bytes=VMEM_LIMIT),
        name="ssd_core",
    )(zx, zx, dt, conv_w, conv_b.reshape(1, SSD_CONV_DIM), a_log_pad, d_e,
      norm_w.reshape(1, SSD_D_INNER), sel, tri)


def _out_proj_kernel(a_ref, w_ref, x_ref, gate_ref, o_ref):
    y = jnp.dot(a_ref[...], w_ref[...], preferred_element_type=F32)
    o_ref[...] = x_ref[...] + gate_ref[0] * y


def _out_proj(a, w, x, gate, tm):
    k = a.shape[1]
    per_batch = SEQ // tm
    return pl.pallas_call(
        _out_proj_kernel,
        grid=(TOKENS // tm,),
        in_specs=[
            pl.BlockSpec((tm, k), lambda i: (i, 0)),
            pl.BlockSpec((k, D_MODEL), lambda i: (0, 0)),
            pl.BlockSpec((tm, D_MODEL), lambda i: (i, 0)),
            pl.BlockSpec((1, 1, D_MODEL), lambda i: (i // per_batch, 0, 0)),
        ],
        out_specs=pl.BlockSpec((tm, D_MODEL), lambda i: (i, 0)),
        out_shape=jax.ShapeDtypeStruct((TOKENS, D_MODEL), F32),
        compiler_params=pltpu.CompilerParams(
            dimension_semantics=("parallel",), vmem_limit_bytes=VMEM_LIMIT),
        name="out_proj",
    )(a, w, x, gate)


def _sc_out_kernel(p_ref, prev_ref, cw_ref, w_ref, x_ref, gate_ref, o_ref, *, per_batch):
    first_tile = (pl.program_id(0) % per_batch) == 0
    cur = p_ref[:, SC_WIDTH:2 * SC_WIDTH] * p_ref[:, 2 * SC_WIDTH:]
    prev8 = prev_ref[:, SC_WIDTH:2 * SC_WIDTH] * prev_ref[:, 2 * SC_WIDTH:]
    prev8 = jnp.where(first_tile, 0.0, prev8)
    acc = cur * cw_ref[SC_CONV_K - 1:SC_CONV_K, :]
    for s in range(1, SC_CONV_K):
        k = SC_CONV_K - 1 - s
        acc = acc + _shift_rows(cur, prev8, s) * cw_ref[k:k + 1, :]
    y = (p_ref[:, :SC_WIDTH] * acc).astype(BF16)
    o_ref[...] = x_ref[...] + gate_ref[0] * jnp.dot(y, w_ref[...], preferred_element_type=F32)


def _sc_out(proj, conv_w, w, x, gate, tm):
    per_batch = SEQ // tm
    blocks8 = tm // SUBLANES
    return pl.pallas_call(
        functools.partial(_sc_out_kernel, per_batch=per_batch),
        grid=(TOKENS // tm,),
        in_specs=[
            pl.BlockSpec((tm, 3 * SC_WIDTH), lambda i: (i, 0)),
            pl.BlockSpec((SUBLANES, 3 * SC_WIDTH), lambda i: (jnp.maximum(i * blocks8 - 1, 0), 0)),
            pl.BlockSpec((SC_CONV_K, SC_WIDTH), lambda i: (0, 0)),
            pl.BlockSpec((SC_WIDTH, D_MODEL), lambda i: (0, 0)),
            pl.BlockSpec((tm, D_MODEL), lambda i: (i, 0)),
            pl.BlockSpec((1, 1, D_MODEL), lambda i: (i // per_batch, 0, 0)),
        ],
        out_specs=pl.BlockSpec((tm, D_MODEL), lambda i: (i, 0)),
        out_shape=jax.ShapeDtypeStruct((TOKENS, D_MODEL), F32),
        compiler_params=pltpu.CompilerParams(
            dimension_semantics=("parallel",), vmem_limit_bytes=VMEM_LIMIT),
        name="sc_out",
    )(proj, proj, conv_w, w, x, gate)


def _mlp_kernel(x_ref, g_ref, sc_ref, sh_ref, gate_ref, up_ref, down_ref, fw_ref, o_ref,
                h_ref, acc_ref, *, final_norm):
    j = pl.program_id(1)

    @pl.when(j == 0)
    def _():
        h_ref[...] = _norm_mod(x_ref[...], g_ref[...], sc_ref[0], sh_ref[0]).astype(BF16)
        acc_ref[...] = jnp.zeros_like(acc_ref)

    a = jnp.maximum(jnp.dot(h_ref[...], up_ref[...], preferred_element_type=F32), 0.0)
    acc_ref[...] += jnp.dot((a * a).astype(BF16), down_ref[...], preferred_element_type=F32)

    @pl.when(j == pl.num_programs(1) - 1)
    def _():
        out = x_ref[...] + gate_ref[0] * acc_ref[...]
        if final_norm:
            ms = jnp.mean(out * out, axis=-1, keepdims=True)
            out = out * lax.rsqrt(ms + NORM_EPS) * fw_ref[...]
        o_ref[...] = out


def _mlp(x, g, sc, sh, gate, up, down, final_w, tm, tf, final_norm):
    per_batch = SEQ // tm
    return pl.pallas_call(
        functools.partial(_mlp_kernel, final_norm=final_norm),
        grid=(TOKENS // tm, D_FF // tf),
        in_specs=[
            pl.BlockSpec((tm, D_MODEL), lambda i, j: (i, 0)),
            pl.BlockSpec((1, D_MODEL), lambda i, j: (0, 0)),
            pl.BlockSpec((1, 1, D_MODEL), lambda i, j: (i // per_batch, 0, 0)),
            pl.BlockSpec((1, 1, D_MODEL), lambda i, j: (i // per_batch, 0, 0)),
            pl.BlockSpec((1, 1, D_MODEL), lambda i, j: (i // per_batch, 0, 0)),
            pl.BlockSpec((D_MODEL, tf), lambda i, j: (0, j)),
            pl.BlockSpec((tf, D_MODEL), lambda i, j: (j, 0)),
            pl.BlockSpec((1, D_MODEL), lambda i, j: (0, 0)),
        ],
        out_specs=pl.BlockSpec((tm, D_MODEL), lambda i, j: (i, 0)),
        out_shape=jax.ShapeDtypeStruct((TOKENS, D_MODEL), F32),
        scratch_shapes=[pltpu.VMEM((tm, D_MODEL), BF16), pltpu.VMEM((tm, D_MODEL), F32)],
        compiler_params=pltpu.CompilerParams(
            dimension_semantics=("parallel", "arbitrary"), vmem_limit_bytes=VMEM_LIMIT),
        name="mlp_final" if final_norm else "mlp",
    )(x, g, sc, sh, gate, up, down, final_w)


def kernel(x, c, ada_w, ada_b, mix_norm_w, mlp_norm_w, mlp_up, mlp_down, ssd_in_w, ssd_conv_w,
           ssd_conv_b, ssd_dt_bias, ssd_A_log, ssd_D, ssd_norm_w, ssd_out_w, sc_in_w, sc_conv_w,
           sc_out_w, final_norm_w):
    xf = x.reshape(TOKENS, D_MODEL)
    mod = _ada_mod(c, ada_w, ada_b).reshape(DEPTH, BATCH, ADA_MODS, 1, D_MODEL)

    def mods(i):
        return [mod[i, :, k] for k in range(ADA_MODS)]

    sh_m, sc_m, g_m, sh_f, sc_f, g_f = mods(0)
    in_w = ssd_in_w[0]
    w_zx = in_w[:, :SSD_ZX_DIM].astype(BF16)
    w_dt = jnp.zeros((D_MODEL, LANES), BF16).at[:, :SSD_N_HEADS].set(
        in_w[:, SSD_ZX_DIM:].astype(BF16))
    dt_bias = jnp.zeros((1, LANES), F32).at[0, :SSD_N_HEADS].set(ssd_dt_bias[0])
    zx, dt = _in_proj(xf, mix_norm_w[0].reshape(1, D_MODEL), sc_m, sh_m, w_zx,
                      tm=1024, tn=1024, wdt=w_dt, dtb=dt_bias)
    yg = _ssd_core(zx, dt, ssd_conv_w[0], ssd_conv_b[0], ssd_A_log[0], ssd_D[0], ssd_norm_w[0])
    xf = _out_proj(yg, ssd_out_w[0].astype(BF16), xf, g_m, tm=1024)
    xf = _mlp(xf, mlp_norm_w[0].reshape(1, D_MODEL), sc_f, sh_f, g_f,
              mlp_up[0].astype(BF16), mlp_down[0].astype(BF16),
              final_norm_w.reshape(1, D_MODEL), tm=1024, tf=1024, final_norm=False)

    sh_m, sc_m, g_m, sh_f, sc_f, g_f = mods(1)
    proj = _in_proj(xf, mix_norm_w[1].reshape(1, D_MODEL), sc_m, sh_m,
                    sc_in_w[0].astype(BF16), tm=1024, tn=1024)
    xf = _sc_out(proj, sc_conv_w[0], sc_out_w[0].astype(BF16), xf, g_m, tm=512)
    xf = _mlp(xf, mlp_norm_w[1].reshape(1, D_MODEL), sc_f, sh_f, g_f,
              mlp_up[1].astype(BF16), mlp_down[1].astype(BF16),
              final_norm_w.reshape(1, D_MODEL), tm=1024, tf=1024, final_norm=True)
    return xf.reshape(BATCH, SEQ, D_MODEL)
```

```python
import functools
import math

import jax
import jax.numpy as jnp
from jax import lax
from jax.experimental import pallas as pl
from jax.experimental.pallas import tpu as pltpu

F32 = jnp.float32
BF16 = jnp.bfloat16

D_MODEL = 1024
BATCH = 8
SEQ = 2048
TOKENS = BATCH * SEQ
DEPTH = 2
NORM_EPS = 1e-5
ADA_MODS = 6

SSD_D_INNER = 2048
SSD_HEAD_DIM = 64
SSD_N_HEADS = 32
SSD_N_GROUPS = 4
SSD_D_STATE = 128
SSD_CONV_K = 4
SSD_CHUNK = 128
SSD_BC = SSD_N_GROUPS * SSD_D_STATE
SSD_CONV_DIM = SSD_D_INNER + 2 * SSD_BC
SSD_ZX_DIM = SSD_D_INNER + SSD_CONV_DIM
GROUP_WIDTH = SSD_D_INNER // SSD_N_GROUPS

SC_WIDTH = D_MODEL
SC_CONV_K = 3
D_FF = 4 * D_MODEL

LANES = 128
SUBLANES = 8
VMEM_LIMIT = 48 * 1024 * 1024
VMEM_LIMIT_FUSED = 56 * 1024 * 1024
LOG2E = math.log2(math.e)

MIX_TILE = 256
IN_PROJ_COLS = 1024
SC_SUB_TILE = 256


def _silu(v):
    return v * jax.nn.sigmoid(v)


def _softplus(v):
    return jnp.maximum(v, 0.0) + jnp.log1p(jnp.exp(-jnp.abs(v)))


def _norm_mod(x, g, sc, sh):
    ms = jnp.mean(x * x, axis=-1, keepdims=True)
    y = x * lax.rsqrt(ms + NORM_EPS)
    return (y * g) * (1.0 + sc) + sh


def _delay1(t):
    r = pltpu.roll(t, 1, 1)
    sub = lax.broadcasted_iota(jnp.int32, t.shape, 1)
    prev_blk = jnp.concatenate([r[:1], r[:-1]], axis=0)
    return jnp.where(sub < 1, prev_blk, r)


def _causal_conv(cur, prev8, w_ref):
    rows, width = cur.shape
    nblk = rows // SUBLANES
    ext = jnp.concatenate([prev8, cur], axis=0).reshape(nblk + 1, SUBLANES, width)
    acc = ext * w_ref[0:1, :]
    for k in range(1, w_ref.shape[0]):
        acc = _delay1(acc) + ext * w_ref[k:k + 1, :]
    return acc[1:].reshape(rows, width)


def _split_bf16(v, terms):
    parts = []
    r = v
    for t in range(terms):
        p = r.astype(BF16)
        parts.append(p)
        if t + 1 < terms:
            r = r - p.astype(F32)
    return jnp.concatenate(parts, axis=1)


def _ada_kernel(c_ref, w_ref, b_ref, o_ref):
    cond = _silu(c_ref[...])
    o_ref[0] = jnp.dot(cond.astype(BF16), w_ref[0].astype(BF16),
                       preferred_element_type=F32) + b_ref[0]


def _ada_mod(c, ada_w, ada_b):
    tn = 1536
    n = ADA_MODS * D_MODEL
    return pl.pallas_call(
        _ada_kernel,
        grid=(DEPTH, n // tn),
        in_specs=[
            pl.BlockSpec((BATCH, D_MODEL), lambda i, j: (0, 0)),
            pl.BlockSpec((1, D_MODEL, tn), lambda i, j: (i, 0, j)),
            pl.BlockSpec((1, 1, tn), lambda i, j: (i, 0, j)),
        ],
        out_specs=pl.BlockSpec((1, BATCH, tn), lambda i, j: (i, 0, j)),
        out_shape=jax.ShapeDtypeStruct((DEPTH, BATCH, n), F32),
        compiler_params=pltpu.CompilerParams(
            dimension_semantics=("parallel", "parallel"), vmem_limit_bytes=VMEM_LIMIT),
        name="ada_mod",
    )(c, ada_w, ada_b.reshape(DEPTH, 1, n))


def _emit_next(tasks):
    if tasks:
        tasks.pop(0)()


def _ssd_chunk(cur, prev8, z_of, dt, cw_ref, cb_ref, alog_ref, dskip_ref, nw_ref, sel_ref, tri_ref,
               state_ref, out_store, side):
    xbc = _silu(_causal_conv(cur, prev8, cw_ref) + cb_ref[...])
    _emit_next(side)
    xs = xbc[:, :SSD_D_INNER]
    bm = xbc[:, SSD_D_INNER:SSD_D_INNER + SSD_BC]
    cm = xbc[:, SSD_D_INNER + SSD_BC:]

    a2 = dt * (-jnp.exp(alog_ref[...]) * LOG2E)
    a2_parts = _split_bf16(a2, 3)
    acs = jnp.dot(tri_ref[...],
                  jnp.concatenate([a2_parts[:, k * LANES:(k + 1) * LANES] for k in range(3)], axis=0),
                  preferred_element_type=F32)
    last = acs[SSD_CHUNK - 1:SSD_CHUNK, :]
    src_t = (acs - jnp.log2(dt)).T
    out_scale_c = jnp.exp2(acs)
    w_state_c = dt * jnp.exp2(last - acs)
    expanded = jnp.dot(jnp.concatenate([_split_bf16(w_state_c, 2), _split_bf16(out_scale_c, 2)], axis=0),
                       sel_ref[...], preferred_element_type=F32)
    w_state = expanded[:SSD_CHUNK]
    out_scale = expanded[SSD_CHUNK:]
    chunk_decay = out_scale[SSD_CHUNK - 1:SSD_CHUNK, :]
    x_state = (xs * w_state).astype(BF16)

    li = lax.broadcasted_iota(jnp.int32, (SSD_CHUNK, SSD_CHUNK), 0)
    si = lax.broadcasted_iota(jnp.int32, (SSD_CHUNK, SSD_CHUNK), 1)
    causal = li >= si
    left_half = si < SSD_HEAD_DIM

    for g in range(SSD_N_GROUPS):
        if g % 2 == 1:
            _emit_next(side)
        gcol = g * GROUP_WIDTH
        b_g = bm[:, g * SSD_D_STATE:(g + 1) * SSD_D_STATE]
        c_g = cm[:, g * SSD_D_STATE:(g + 1) * SSD_D_STATE].astype(BF16)
        scores = lax.dot_general(c_g, b_g.astype(BF16), (((1,), (1,)), ((), ())),
                                 preferred_element_type=F32)
        state = state_ref[g]
        y_off = jnp.dot(c_g, state.astype(BF16), preferred_element_type=F32)
        y_off = y_off * out_scale[:, gcol:gcol + GROUP_WIDTH]
        new_state = jnp.dot(b_g.T.astype(BF16), x_state[:, gcol:gcol + GROUP_WIDTH],
                            preferred_element_type=F32)
        state_ref[g] = state * chunk_decay[:, gcol:gcol + GROUP_WIDTH] + new_state

        y_parts = []
        for pair in range(GROUP_WIDTH // LANES):
            col = gcol + pair * LANES
            h0 = col // SSD_HEAD_DIM
            m = []
            for h in (h0, h0 + 1):
                seg = acs[:, h:h + 1] - src_t[h:h + 1, :]
                m.append((scores * jnp.where(causal, jnp.exp2(seg), 0.0)).astype(BF16))
            lhs = jnp.concatenate(m, axis=1)
            xp = xs[:, col:col + LANES]
            rhs = jnp.concatenate([jnp.where(left_half, xp, 0.0),
                                   jnp.where(left_half, 0.0, xp)], axis=0).astype(BF16)
            y_diag = jnp.dot(lhs, rhs, preferred_element_type=F32)
            y_parts.append(y_diag + y_off[:, pair * LANES:(pair + 1) * LANES]
                           + dskip_ref[:, col:col + LANES] * xp)
        y = jnp.concatenate(y_parts, axis=1)

        yg = y * _silu(z_of(gcol))
        ms = jnp.mean(yg * yg, axis=-1, keepdims=True)
        yn = yg * lax.rsqrt(ms + NORM_EPS) * nw_ref[:, gcol:gcol + GROUP_WIDTH]
        out_store(gcol, yn.astype(BF16))


def _mixer0_kernel(xb_ref, xa1_ref, xa2_ref, g_ref, sc_ref, sh_ref, gate_ref,
                   wzx_ref, wdt_ref, dtbias_ref, wout_ref,
                   cw_ref, cb_ref, alog_ref, dskip_ref, nw_ref, sel_ref, tri_ref,
                   o_ref,
                   zx_a, zx_b, dt_a, dt_b, state_ref, tail_ref, yg_ref):
    i = pl.program_id(0)
    tiles_per_batch = SEQ // MIX_TILE
    last_tile = TOKENS // MIX_TILE - 1

    def in_proj_tasks(x_ref, batch, zx_out, dt_out):
        cell = []

        def head():
            h = _norm_mod(x_ref[...], g_ref[...], sc_ref[batch], sh_ref[batch]).astype(BF16)
            cell.append(h)
            raw = jnp.dot(h, wdt_ref[...], preferred_element_type=F32) + dtbias_ref[...]
            dt_out[...] = _softplus(raw)

        def cols(j):
            def task():
                c0 = j * IN_PROJ_COLS
                zx_out[:, c0:c0 + IN_PROJ_COLS] = jnp.dot(
                    cell[0], wzx_ref[:, c0:c0 + IN_PROJ_COLS], preferred_element_type=F32)
            return task

        return [head] + [cols(j) for j in range(SSD_ZX_DIM // IN_PROJ_COLS)]

    def mix(zx_in, dt_in, first, rows, side):
        for k in range(MIX_TILE // SSD_CHUNK):
            r0 = k * SSD_CHUNK
            cur = zx_in[r0:r0 + SSD_CHUNK, SSD_D_INNER:]
            if k == 0:
                prev8 = tail_ref[...] if first is None else jnp.where(first, 0.0, tail_ref[...])
            else:
                prev8 = zx_in[r0 - SUBLANES:r0, SSD_D_INNER:]

            def z_of(gcol, r0=r0):
                return zx_in[r0:r0 + SSD_CHUNK, gcol:gcol + GROUP_WIDTH]

            def out_store(gcol, v, r0=r0):
                yg_ref[r0:r0 + SSD_CHUNK, gcol:gcol + GROUP_WIDTH] = v

            _ssd_chunk(cur, prev8, z_of, dt_in[r0:r0 + SSD_CHUNK, :], cw_ref, cb_ref, alog_ref,
                       dskip_ref, nw_ref, sel_ref, tri_ref, state_ref, out_store, side)
        while side:
            _emit_next(side)
        tail_ref[...] = zx_in[MIX_TILE - SUBLANES:, SSD_D_INNER:]
        y = jnp.dot(yg_ref[...], wout_ref[...], preferred_element_type=F32)
        o_ref[rows, :] = xb_ref[rows, :] + gate_ref[i // (tiles_per_batch // 2)] * y

    @pl.when(i == 0)
    def _():
        tail_ref[...] = jnp.zeros_like(tail_ref)
        for task in in_proj_tasks(xb_ref.at[0:MIX_TILE, :], 0, zx_a, dt_a):
            task()

    first = (i % (tiles_per_batch // 2)) == 0

    @pl.when(first)
    def _():
        state_ref[...] = jnp.zeros_like(state_ref)

    mix(zx_a, dt_a, first, slice(0, MIX_TILE),
        in_proj_tasks(xa1_ref, (2 * i + 1) // tiles_per_batch, zx_b, dt_b))
    mix(zx_b, dt_b, None, slice(MIX_TILE, 2 * MIX_TILE),
        in_proj_tasks(xa2_ref, jnp.minimum(2 * i + 2, last_tile) // tiles_per_batch, zx_a, dt_a))


def _mixer0(x, g, sc, sh, gate, w_zx, w_dt, dt_bias, w_out, conv_w, conv_b, a_log, d_skip, norm_w):
    n_tiles = TOKENS // MIX_TILE
    head_of_lane = jnp.arange(SSD_D_INNER) // SSD_HEAD_DIM
    sel = (jnp.arange(LANES)[:, None] == head_of_lane[None, :]).astype(BF16)
    sel2 = jnp.concatenate([sel, sel], axis=0)
    tri = (jnp.arange(SSD_CHUNK)[:, None] >= jnp.arange(SSD_CHUNK)[None, :]).astype(BF16)
    tri3 = jnp.concatenate([tri, tri, tri], axis=1)
    a_log_pad = jnp.zeros((1, LANES), F32).at[0, :SSD_N_HEADS].set(a_log)
    d_e = jnp.repeat(d_skip, SSD_HEAD_DIM).reshape(1, SSD_D_INNER)

    def resident(shape):
        return pl.BlockSpec(shape, lambda i: (0,) * len(shape), pipeline_mode=pl.Buffered(1))

    return pl.pallas_call(
        _mixer0_kernel,
        grid=(n_tiles // 2,),
        in_specs=[
            pl.BlockSpec((2 * MIX_TILE, D_MODEL), lambda i: (i, 0)),
            pl.BlockSpec((MIX_TILE, D_MODEL), lambda i: (2 * i + 1, 0)),
            pl.BlockSpec((MIX_TILE, D_MODEL), lambda i: (jnp.minimum(2 * i + 2, n_tiles - 1), 0)),
            resident((1, D_MODEL)),
            resident((BATCH, 1, D_MODEL)),
            resident((BATCH, 1, D_MODEL)),
            resident((BATCH, 1, D_MODEL)),
            resident((D_MODEL, SSD_ZX_DIM)),
            resident((D_MODEL, LANES)),
            resident((1, LANES)),
            resident((SSD_D_INNER, D_MODEL)),
            resident((SSD_CONV_K, SSD_CONV_DIM)),
            resident((1, SSD_CONV_DIM)),
            resident((1, LANES)),
            resident((1, SSD_D_INNER)),
            resident((1, SSD_D_INNER)),
            resident((2 * LANES, SSD_D_INNER)),
            resident((SSD_CHUNK, 3 * SSD_CHUNK)),
        ],
        out_specs=pl.BlockSpec((2 * MIX_TILE, D_MODEL), lambda i: (i, 0)),
        out_shape=jax.ShapeDtypeStruct((TOKENS, D_MODEL), F32),
        scratch_shapes=[
            pltpu.VMEM((MIX_TILE, SSD_ZX_DIM), F32),
            pltpu.VMEM((MIX_TILE, SSD_ZX_DIM), F32),
            pltpu.VMEM((MIX_TILE, LANES), F32),
            pltpu.VMEM((MIX_TILE, LANES), F32),
            pltpu.VMEM((SSD_N_GROUPS, SSD_D_STATE, GROUP_WIDTH), F32),
            pltpu.VMEM((SUBLANES, SSD_CONV_DIM), F32),
            pltpu.VMEM((MIX_TILE, SSD_D_INNER), BF16),
        ],
        compiler_params=pltpu.CompilerParams(
            dimension_semantics=("arbitrary",), vmem_limit_bytes=VMEM_LIMIT_FUSED),
        name="ssd_mixer",
    )(x, x, x, g, sc, sh, gate, w_zx, w_dt, dt_bias, w_out, conv_w, conv_b.reshape(1, SSD_CONV_DIM),
      a_log_pad, d_e, norm_w.reshape(1, SSD_D_INNER), sel2, tri3)


def _mixer1_kernel(x_ref, g_ref, sc_ref, sh_ref, gate_ref, win_ref, cw_ref, wout_ref, o_ref, tail_ref,
                   *, per_batch):
    first_tile = (pl.program_id(0) % per_batch) == 0

    @pl.when(pl.program_id(0) == 0)
    def _():
        tail_ref[...] = jnp.zeros_like(tail_ref)

    u_tail = jnp.where(first_tile, 0.0, tail_ref[...])
    rows = x_ref.shape[0]
    for r0 in range(0, rows, SC_SUB_TILE):
        x = x_ref[r0:r0 + SC_SUB_TILE, :]
        h = _norm_mod(x, g_ref[...], sc_ref[0], sh_ref[0]).astype(BF16)
        proj = jnp.dot(h, win_ref[...], preferred_element_type=F32)
        u = proj[:, SC_WIDTH:2 * SC_WIDTH] * proj[:, 2 * SC_WIDTH:]
        y = (proj[:, :SC_WIDTH] * _causal_conv(u, u_tail, cw_ref)).astype(BF16)
        u_tail = u[SC_SUB_TILE - SUBLANES:, :]
        o_ref[r0:r0 + SC_SUB_TILE, :] = x + gate_ref[0] * jnp.dot(
            y, wout_ref[...], preferred_element_type=F32)
    tail_ref[...] = u_tail


def _mixer1(x, g, sc, sh, gate, w_in, conv_w, w_out, tm):
    per_batch = SEQ // tm
    return pl.pallas_call(
        functools.partial(_mixer1_kernel, per_batch=per_batch),
        grid=(TOKENS // tm,),
        in_specs=[
            pl.BlockSpec((tm, D_MODEL), lambda i: (i, 0)),
            pl.BlockSpec((1, D_MODEL), lambda i: (0, 0)),
            pl.BlockSpec((1, 1, D_MODEL), lambda i: (i // per_batch, 0, 0)),
            pl.BlockSpec((1, 1, D_MODEL), lambda i: (i // per_batch, 0, 0)),
            pl.BlockSpec((1, 1, D_MODEL), lambda i: (i // per_batch, 0, 0)),
            pl.BlockSpec((D_MODEL, 3 * SC_WIDTH), lambda i: (0, 0)),
            pl.BlockSpec((SC_CONV_K, SC_WIDTH), lambda i: (0, 0)),
            pl.BlockSpec((SC_WIDTH, D_MODEL), lambda i: (0, 0)),
        ],
        out_specs=pl.BlockSpec((tm, D_MODEL), lambda i: (i, 0)),
        out_shape=jax.ShapeDtypeStruct((TOKENS, D_MODEL), F32),
        scratch_shapes=[pltpu.VMEM((SUBLANES, SC_WIDTH), F32)],
        compiler_params=pltpu.CompilerParams(
            dimension_semantics=("arbitrary",), vmem_limit_bytes=VMEM_LIMIT),
        name="sc_mixer",
    )(x, g, sc, sh, gate, w_in, conv_w, w_out)


def _mlp_kernel(x_ref, g_ref, sc_ref, sh_ref, gate_ref, up_ref, down_ref, fw_ref, o_ref,
                h_ref, acc_ref, *, final_norm):
    j = pl.program_id(1)

    @pl.when(j == 0)
    def _():
        h_ref[...] = _norm_mod(x_ref[...], g_ref[...], sc_ref[0], sh_ref[0]).astype(BF16)
        acc_ref[...] = jnp.zeros_like(acc_ref)

    a = jnp.maximum(jnp.dot(h_ref[...], up_ref[...], preferred_element_type=F32), 0.0)
    acc_ref[...] += jnp.dot((a * a).astype(BF16), down_ref[...], preferred_element_type=F32)

    @pl.when(j == pl.num_programs(1) - 1)
    def _():
        out = x_ref[...] + gate_ref[0] * acc_ref[...]
        if final_norm:
            ms = jnp.mean(out * out, axis=-1, keepdims=True)
            out = out * lax.rsqrt(ms + NORM_EPS) * fw_ref[...]
        o_ref[...] = out


def _mlp(x, g, sc, sh, gate, up, down, final_w, tm, tf, final_norm):
    per_batch = SEQ // tm
    return pl.pallas_call(
        functools.partial(_mlp_kernel, final_norm=final_norm),
        grid=(TOKENS // tm, D_FF // tf),
        in_specs=[
            pl.BlockSpec((tm, D_MODEL), lambda i, j: (i, 0)),
            pl.BlockSpec((1, D_MODEL), lambda i, j: (0, 0)),
            pl.BlockSpec((1, 1, D_MODEL), lambda i, j: (i // per_batch, 0, 0)),
            pl.BlockSpec((1, 1, D_MODEL), lambda i, j: (i // per_batch, 0, 0)),
            pl.BlockSpec((1, 1, D_MODEL), lambda i, j: (i // per_batch, 0, 0)),
            pl.BlockSpec((D_MODEL, tf), lambda i, j: (0, j)),
            pl.BlockSpec((tf, D_MODEL), lambda i, j: (j, 0)),
            pl.BlockSpec((1, D_MODEL), lambda i, j: (0, 0)),
        ],
        out_specs=pl.BlockSpec((tm, D_MODEL), lambda i, j: (i, 0)),
        out_shape=jax.ShapeDtypeStruct((TOKENS, D_MODEL), F32),
        scratch_shapes=[pltpu.VMEM((tm, D_MODEL), BF16), pltpu.VMEM((tm, D_MODEL), F32)],
        compiler_params=pltpu.CompilerParams(
            dimension_semantics=("parallel", "arbitrary"), vmem_limit_bytes=VMEM_LIMIT),
        name="mlp_final" if final_norm else "mlp",
    )(x, g, sc, sh, gate, up, down, final_w)


def kernel(x, c, ada_w, ada_b, mix_norm_w, mlp_norm_w, mlp_up, mlp_down, ssd_in_w, ssd_conv_w,
           ssd_conv_b, ssd_dt_bias, ssd_A_log, ssd_D, ssd_norm_w, ssd_out_w, sc_in_w, sc_conv_w,
           sc_out_w, final_norm_w):
    xf = x.reshape(TOKENS, D_MODEL)
    mod = _ada_mod(c, ada_w, ada_b).reshape(DEPTH, BATCH, ADA_MODS, 1, D_MODEL)

    def mods(i):
        return [mod[i, :, k] for k in range(ADA_MODS)]

    sh_m, sc_m, g_m, sh_f, sc_f, g_f = mods(0)
    in_w = ssd_in_w[0]
    w_zx = in_w[:, :SSD_ZX_DIM].astype(BF16)
    w_dt = jnp.zeros((D_MODEL, LANES), BF16).at[:, :SSD_N_HEADS].set(
        in_w[:, SSD_ZX_DIM:].astype(BF16))
    dt_bias = jnp.zeros((1, LANES), F32).at[0, :SSD_N_HEADS].set(ssd_dt_bias[0])
    xf = _mixer0(xf, mix_norm_w[0].reshape(1, D_MODEL), sc_m, sh_m, g_m, w_zx, w_dt, dt_bias,
                 ssd_out_w[0].astype(BF16), ssd_conv_w[0], ssd_conv_b[0], ssd_A_log[0], ssd_D[0],
                 ssd_norm_w[0])
    xf = _mlp(xf, mlp_norm_w[0].reshape(1, D_MODEL), sc_f, sh_f, g_f,
              mlp_up[0].astype(BF16), mlp_down[0].astype(BF16),
              final_norm_w.reshape(1, D_MODEL), tm=1024, tf=1024, final_norm=False)

    sh_m, sc_m, g_m, sh_f, sc_f, g_f = mods(1)
    xf = _mixer1(xf, mix_norm_w[1].reshape(1, D_MODEL), sc_m, sh_m, g_m, sc_in_w[0].astype(BF16),
                 sc_conv_w[0], sc_out_w[0].astype(BF16), tm=1024)
    xf = _mlp(xf, mlp_norm_w[1].reshape(1, D_MODEL), sc_f, sh_f, g_f,
              mlp_up[1].astype(BF16), mlp_down[1].astype(BF16),
              final_norm_w.reshape(1, D_MODEL), tm=1024, tf=1024, final_norm=True)
    return xf.reshape(BATCH, SEQ, D_MODEL)
```

```python
import functools
import math

import jax
import jax.numpy as jnp
from jax import lax
from jax.experimental import pallas as pl
from jax.experimental.pallas import tpu as pltpu

F32 = jnp.float32
BF16 = jnp.bfloat16

D_MODEL = 1024
BATCH = 8
SEQ = 2048
TOKENS = BATCH * SEQ
DEPTH = 2
NORM_EPS = 1e-5
ADA_MODS = 6

SSD_D_INNER = 2048
SSD_HEAD_DIM = 64
SSD_N_HEADS = 32
SSD_N_GROUPS = 4
SSD_D_STATE = 128
SSD_CONV_K = 4
SSD_CHUNK = 128
SSD_BC = SSD_N_GROUPS * SSD_D_STATE
SSD_CONV_DIM = SSD_D_INNER + 2 * SSD_BC
SSD_ZX_DIM = SSD_D_INNER + SSD_CONV_DIM
GROUP_WIDTH = SSD_D_INNER // SSD_N_GROUPS

SC_WIDTH = D_MODEL
SC_CONV_K = 3
D_FF = 4 * D_MODEL

LANES = 128
SUBLANES = 8
VMEM_LIMIT = 48 * 1024 * 1024
VMEM_LIMIT_FUSED = 56 * 1024 * 1024
LOG2E = math.log2(math.e)

MIX_TILE = 256
IN_PROJ_COLS = 1024
SC_SUB_TILE = 256
MLP_FF_CHUNK = 1024


def _silu(v):
    return v * jax.nn.sigmoid(v)


def _softplus(v):
    return jnp.maximum(v, 0.0) + jnp.log1p(jnp.exp(-jnp.abs(v)))


def _norm_mod(x, g, sc, sh):
    ms = jnp.mean(x * x, axis=-1, keepdims=True)
    y = x * lax.rsqrt(ms + NORM_EPS)
    return (y * g) * (1.0 + sc) + sh


def _delay1(t):
    r = pltpu.roll(t, 1, 1)
    sub = lax.broadcasted_iota(jnp.int32, t.shape, 1)
    prev_blk = jnp.concatenate([r[:1], r[:-1]], axis=0)
    return jnp.where(sub < 1, prev_blk, r)


def _causal_conv(cur, prev8, w_ref):
    rows, width = cur.shape
    nblk = rows // SUBLANES
    ext = jnp.concatenate([prev8, cur], axis=0).reshape(nblk + 1, SUBLANES, width)
    acc = ext * w_ref[0:1, :]
    for k in range(1, w_ref.shape[0]):
        acc = _delay1(acc) + ext * w_ref[k:k + 1, :]
    return acc[1:].reshape(rows, width)


def _split_bf16(v, terms):
    parts = []
    r = v
    for t in range(terms):
        p = r.astype(BF16)
        parts.append(p)
        if t + 1 < terms:
            r = r - p.astype(F32)
    return jnp.concatenate(parts, axis=1)


def _ada_kernel(c_ref, w_ref, b_ref, o_ref):
    cond = _silu(c_ref[...])
    o_ref[0] = jnp.dot(cond.astype(BF16), w_ref[0].astype(BF16),
                       preferred_element_type=F32) + b_ref[0]


def _ada_mod(c, ada_w, ada_b):
    tn = 1536
    n = ADA_MODS * D_MODEL
    return pl.pallas_call(
        _ada_kernel,
        grid=(DEPTH, n // tn),
        in_specs=[
            pl.BlockSpec((BATCH, D_MODEL), lambda i, j: (0, 0)),
            pl.BlockSpec((1, D_MODEL, tn), lambda i, j: (i, 0, j)),
            pl.BlockSpec((1, 1, tn), lambda i, j: (i, 0, j)),
        ],
        out_specs=pl.BlockSpec((1, BATCH, tn), lambda i, j: (i, 0, j)),
        out_shape=jax.ShapeDtypeStruct((DEPTH, BATCH, n), F32),
        compiler_params=pltpu.CompilerParams(
            dimension_semantics=("parallel", "parallel"), vmem_limit_bytes=VMEM_LIMIT),
        name="ada_mod",
    )(c, ada_w, ada_b.reshape(DEPTH, 1, n))


def _emit_next(tasks):
    if tasks:
        tasks.pop(0)()


def _ssd_chunk(cur, prev8, z_of, dt, cw_ref, cb_ref, alog_ref, dskip_ref, nw_ref, sel_ref, tri_ref,
               state_ref, out_store, side):
    xbc = _silu(_causal_conv(cur, prev8, cw_ref) + cb_ref[...])
    _emit_next(side)
    xs = xbc[:, :SSD_D_INNER]
    bm = xbc[:, SSD_D_INNER:SSD_D_INNER + SSD_BC]
    cm = xbc[:, SSD_D_INNER + SSD_BC:]

    a2 = dt * (-jnp.exp(alog_ref[...]) * LOG2E)
    a2_parts = _split_bf16(a2, 3)
    acs = jnp.dot(tri_ref[...],
                  jnp.concatenate([a2_parts[:, k * LANES:(k + 1) * LANES] for k in range(3)], axis=0),
                  preferred_element_type=F32)
    last = acs[SSD_CHUNK - 1:SSD_CHUNK, :]
    src_t = (acs - jnp.log2(dt)).T
    out_scale_c = jnp.exp2(acs)
    w_state_c = dt * jnp.exp2(last - acs)
    expanded = jnp.dot(jnp.concatenate([_split_bf16(w_state_c, 2), _split_bf16(out_scale_c, 2)], axis=0),
                       sel_ref[...], preferred_element_type=F32)
    w_state = expanded[:SSD_CHUNK]
    out_scale = expanded[SSD_CHUNK:]
    chunk_decay = out_scale[SSD_CHUNK - 1:SSD_CHUNK, :]
    x_state = (xs * w_state).astype(BF16)

    li = lax.broadcasted_iota(jnp.int32, (SSD_CHUNK, SSD_CHUNK), 0)
    si = lax.broadcasted_iota(jnp.int32, (SSD_CHUNK, SSD_CHUNK), 1)
    causal = li >= si
    left_half = si < SSD_HEAD_DIM

    for g in range(SSD_N_GROUPS):
        if g % 2 == 1:
            _emit_next(side)
        gcol = g * GROUP_WIDTH
        b_g = bm[:, g * SSD_D_STATE:(g + 1) * SSD_D_STATE]
        c_g = cm[:, g * SSD_D_STATE:(g + 1) * SSD_D_STATE].astype(BF16)
        scores = lax.dot_general(c_g, b_g.astype(BF16), (((1,), (1,)), ((), ())),
                                 preferred_element_type=F32)
        state = state_ref[g]
        y_off = jnp.dot(c_g, state.astype(BF16), preferred_element_type=F32)
        y_off = y_off * out_scale[:, gcol:gcol + GROUP_WIDTH]
        new_state = jnp.dot(b_g.T.astype(BF16), x_state[:, gcol:gcol + GROUP_WIDTH],
                            preferred_element_type=F32)
        state_ref[g] = state * chunk_decay[:, gcol:gcol + GROUP_WIDTH] + new_state

        y_parts = []
        for pair in range(GROUP_WIDTH // LANES):
            col = gcol + pair * LANES
            h0 = col // SSD_HEAD_DIM
            m = []
            for h in (h0, h0 + 1):
                seg = acs[:, h:h + 1] - src_t[h:h + 1, :]
                m.append((scores * jnp.where(causal, jnp.exp2(seg), 0.0)).astype(BF16))
            lhs = jnp.concatenate(m, axis=1)
            xp = xs[:, col:col + LANES]
            rhs = jnp.concatenate([jnp.where(left_half, xp, 0.0),
                                   jnp.where(left_half, 0.0, xp)], axis=0).astype(BF16)
            y_diag = jnp.dot(lhs, rhs, preferred_element_type=F32)
            y_parts.append(y_diag + y_off[:, pair * LANES:(pair + 1) * LANES]
                           + dskip_ref[:, col:col + LANES] * xp)
        y = jnp.concatenate(y_parts, axis=1)

        yg = y * _silu(z_of(gcol))
        ms = jnp.mean(yg * yg, axis=-1, keepdims=True)
        yn = yg * lax.rsqrt(ms + NORM_EPS) * nw_ref[:, gcol:gcol + GROUP_WIDTH]
        out_store(gcol, yn.astype(BF16))


def _mixer0_kernel(xb_ref, xa1_ref, xa2_ref, g_ref, sc_ref, sh_ref, gate_ref,
                   wzx_ref, wdt_ref, dtbias_ref, wout_ref,
                   cw_ref, cb_ref, alog_ref, dskip_ref, nw_ref, sel_ref, tri_ref,
                   o_ref,
                   zx_a, zx_b, dt_a, dt_b, state_ref, tail_ref, yg_ref):
    i = pl.program_id(0)
    tiles_per_batch = SEQ // MIX_TILE
    last_tile = TOKENS // MIX_TILE - 1

    def in_proj_tasks(x_ref, batch, zx_out, dt_out):
        cell = []

        def head():
            h = _norm_mod(x_ref[...], g_ref[...], sc_ref[batch], sh_ref[batch]).astype(BF16)
            cell.append(h)
            raw = jnp.dot(h, wdt_ref[...], preferred_element_type=F32) + dtbias_ref[...]
            dt_out[...] = _softplus(raw)

        def cols(j):
            def task():
                c0 = j * IN_PROJ_COLS
                zx_out[:, c0:c0 + IN_PROJ_COLS] = jnp.dot(
                    cell[0], wzx_ref[:, c0:c0 + IN_PROJ_COLS], preferred_element_type=F32)
            return task

        return [head] + [cols(j) for j in range(SSD_ZX_DIM // IN_PROJ_COLS)]

    def mix(zx_in, dt_in, first, rows, side):
        for k in range(MIX_TILE // SSD_CHUNK):
            r0 = k * SSD_CHUNK
            cur = zx_in[r0:r0 + SSD_CHUNK, SSD_D_INNER:]
            if k == 0:
                prev8 = tail_ref[...] if first is None else jnp.where(first, 0.0, tail_ref[...])
            else:
                prev8 = zx_in[r0 - SUBLANES:r0, SSD_D_INNER:]

            def z_of(gcol, r0=r0):
                return zx_in[r0:r0 + SSD_CHUNK, gcol:gcol + GROUP_WIDTH]

            def out_store(gcol, v, r0=r0):
                yg_ref[r0:r0 + SSD_CHUNK, gcol:gcol + GROUP_WIDTH] = v

            _ssd_chunk(cur, prev8, z_of, dt_in[r0:r0 + SSD_CHUNK, :], cw_ref, cb_ref, alog_ref,
                       dskip_ref, nw_ref, sel_ref, tri_ref, state_ref, out_store, side)
        while side:
            _emit_next(side)
        tail_ref[...] = zx_in[MIX_TILE - SUBLANES:, SSD_D_INNER:]
        y = jnp.dot(yg_ref[...], wout_ref[...], preferred_element_type=F32)
        o_ref[rows, :] = xb_ref[rows, :] + gate_ref[i // (tiles_per_batch // 2)] * y

    @pl.when(i == 0)
    def _():
        tail_ref[...] = jnp.zeros_like(tail_ref)
        for task in in_proj_tasks(xb_ref.at[0:MIX_TILE, :], 0, zx_a, dt_a):
            task()

    first = (i % (tiles_per_batch // 2)) == 0

    @pl.when(first)
    def _():
        state_ref[...] = jnp.zeros_like(state_ref)

    mix(zx_a, dt_a, first, slice(0, MIX_TILE),
        in_proj_tasks(xa1_ref, (2 * i + 1) // tiles_per_batch, zx_b, dt_b))
    mix(zx_b, dt_b, None, slice(MIX_TILE, 2 * MIX_TILE),
        in_proj_tasks(xa2_ref, jnp.minimum(2 * i + 2, last_tile) // tiles_per_batch, zx_a, dt_a))


def _mixer0(x, g, sc, sh, gate, w_zx, w_dt, dt_bias, w_out, conv_w, conv_b, a_log, d_skip, norm_w):
    n_tiles = TOKENS // MIX_TILE
    head_of_lane = jnp.arange(SSD_D_INNER) // SSD_HEAD_DIM
    sel = (jnp.arange(LANES)[:, None] == head_of_lane[None, :]).astype(BF16)
    sel2 = jnp.concatenate([sel, sel], axis=0)
    tri = (jnp.arange(SSD_CHUNK)[:, None] >= jnp.arange(SSD_CHUNK)[None, :]).astype(BF16)
    tri3 = jnp.concatenate([tri, tri, tri], axis=1)
    a_log_pad = jnp.zeros((1, LANES), F32).at[0, :SSD_N_HEADS].set(a_log)
    d_e = jnp.repeat(d_skip, SSD_HEAD_DIM).reshape(1, SSD_D_INNER)

    def resident(shape):
        return pl.BlockSpec(shape, lambda i: (0,) * len(shape), pipeline_mode=pl.Buffered(1))

    return pl.pallas_call(
        _mixer0_kernel,
        grid=(n_tiles // 2,),
        in_specs=[
            pl.BlockSpec((2 * MIX_TILE, D_MODEL), lambda i: (i, 0)),
            pl.BlockSpec((MIX_TILE, D_MODEL), lambda i: (2 * i + 1, 0)),
            pl.BlockSpec((MIX_TILE, D_MODEL), lambda i: (jnp.minimum(2 * i + 2, n_tiles - 1), 0)),
            resident((1, D_MODEL)),
            resident((BATCH, 1, D_MODEL)),
            resident((BATCH, 1, D_MODEL)),
            resident((BATCH, 1, D_MODEL)),
            resident((D_MODEL, SSD_ZX_DIM)),
            resident((D_MODEL, LANES)),
            resident((1, LANES)),
            resident((SSD_D_INNER, D_MODEL)),
            resident((SSD_CONV_K, SSD_CONV_DIM)),
            resident((1, SSD_CONV_DIM)),
            resident((1, LANES)),
            resident((1, SSD_D_INNER)),
            resident((1, SSD_D_INNER)),
            resident((2 * LANES, SSD_D_INNER)),
            resident((SSD_CHUNK, 3 * SSD_CHUNK)),
        ],
        out_specs=pl.BlockSpec((2 * MIX_TILE, D_MODEL), lambda i: (i, 0)),
        out_shape=jax.ShapeDtypeStruct((TOKENS, D_MODEL), F32),
        scratch_shapes=[
            pltpu.VMEM((MIX_TILE, SSD_ZX_DIM), F32),
            pltpu.VMEM((MIX_TILE, SSD_ZX_DIM), F32),
            pltpu.VMEM((MIX_TILE, LANES), F32),
            pltpu.VMEM((MIX_TILE, LANES), F32),
            pltpu.VMEM((SSD_N_GROUPS, SSD_D_STATE, GROUP_WIDTH), F32),
            pltpu.VMEM((SUBLANES, SSD_CONV_DIM), F32),
            pltpu.VMEM((MIX_TILE, SSD_D_INNER), BF16),
        ],
        compiler_params=pltpu.CompilerParams(
            dimension_semantics=("arbitrary",), vmem_limit_bytes=VMEM_LIMIT_FUSED),
        name="ssd_mixer",
    )(x, x, x, g, sc, sh, gate, w_zx, w_dt, dt_bias, w_out, conv_w, conv_b.reshape(1, SSD_CONV_DIM),
      a_log_pad, d_e, norm_w.reshape(1, SSD_D_INNER), sel2, tri3)


def _mixer1_kernel(x_ref, g_ref, sc_ref, sh_ref, gate_ref, win_ref, cw_ref, wout_ref, o_ref, tail_ref,
                   *, per_batch):
    first_tile = (pl.program_id(0) % per_batch) == 0

    @pl.when(pl.program_id(0) == 0)
    def _():
        tail_ref[...] = jnp.zeros_like(tail_ref)

    u_tail = jnp.where(first_tile, 0.0, tail_ref[...])
    rows = x_ref.shape[0]
    for r0 in range(0, rows, SC_SUB_TILE):
        x = x_ref[r0:r0 + SC_SUB_TILE, :]
        h = _norm_mod(x, g_ref[...], sc_ref[0], sh_ref[0]).astype(BF16)
        proj = jnp.dot(h, win_ref[...], preferred_element_type=F32)
        u = proj[:, SC_WIDTH:2 * SC_WIDTH] * proj[:, 2 * SC_WIDTH:]
        y = (proj[:, :SC_WIDTH] * _causal_conv(u, u_tail, cw_ref)).astype(BF16)
        u_tail = u[SC_SUB_TILE - SUBLANES:, :]
        o_ref[r0:r0 + SC_SUB_TILE, :] = x + gate_ref[0] * jnp.dot(
            y, wout_ref[...], preferred_element_type=F32)
    tail_ref[...] = u_tail


def _mixer1(x, g, sc, sh, gate, w_in, conv_w, w_out, tm):
    per_batch = SEQ // tm
    return pl.pallas_call(
        functools.partial(_mixer1_kernel, per_batch=per_batch),
        grid=(TOKENS // tm,),
        in_specs=[
            pl.BlockSpec((tm, D_MODEL), lambda i: (i, 0)),
            pl.BlockSpec((1, D_MODEL), lambda i: (0, 0)),
            pl.BlockSpec((1, 1, D_MODEL), lambda i: (i // per_batch, 0, 0)),
            pl.BlockSpec((1, 1, D_MODEL), lambda i: (i // per_batch, 0, 0)),
            pl.BlockSpec((1, 1, D_MODEL), lambda i: (i // per_batch, 0, 0)),
            pl.BlockSpec((D_MODEL, 3 * SC_WIDTH), lambda i: (0, 0)),
            pl.BlockSpec((SC_CONV_K, SC_WIDTH), lambda i: (0, 0)),
            pl.BlockSpec((SC_WIDTH, D_MODEL), lambda i: (0, 0)),
        ],
        out_specs=pl.BlockSpec((tm, D_MODEL), lambda i: (i, 0)),
        out_shape=jax.ShapeDtypeStruct((TOKENS, D_MODEL), F32),
        scratch_shapes=[pltpu.VMEM((SUBLANES, SC_WIDTH), F32)],
        compiler_params=pltpu.CompilerParams(
            dimension_semantics=("arbitrary",), vmem_limit_bytes=VMEM_LIMIT),
        name="sc_mixer",
    )(x, g, sc, sh, gate, w_in, conv_w, w_out)


def _mlp_kernel(x_ref, g_ref, sc_ref, sh_ref, gate_ref, up_ref, down_ref, fw_ref, o_ref, *, final_norm):
    h = _norm_mod(x_ref[...], g_ref[...], sc_ref[0], sh_ref[0]).astype(BF16)
    acc = None
    for c0 in range(0, D_FF, MLP_FF_CHUNK):
        a = jnp.maximum(jnp.dot(h, up_ref[:, c0:c0 + MLP_FF_CHUNK], preferred_element_type=F32), 0.0)
        part = jnp.dot((a * a).astype(BF16), down_ref[c0:c0 + MLP_FF_CHUNK, :],
                       preferred_element_type=F32)
        acc = part if acc is None else acc + part
    out = x_ref[...] + gate_ref[0] * acc
    if final_norm:
        ms = jnp.mean(out * out, axis=-1, keepdims=True)
        out = out * lax.rsqrt(ms + NORM_EPS) * fw_ref[...]
    o_ref[...] = out


def _mlp(x, g, sc, sh, gate, up, down, layer, final_w, tm, final_norm):
    per_batch = SEQ // tm
    return pl.pallas_call(
        functools.partial(_mlp_kernel, final_norm=final_norm),
        grid=(TOKENS // tm,),
        in_specs=[
            pl.BlockSpec((tm, D_MODEL), lambda i: (i, 0)),
            pl.BlockSpec((1, D_MODEL), lambda i: (0, 0)),
            pl.BlockSpec((1, 1, D_MODEL), lambda i: (i // per_batch, 0, 0)),
            pl.BlockSpec((1, 1, D_MODEL), lambda i: (i // per_batch, 0, 0)),
            pl.BlockSpec((1, 1, D_MODEL), lambda i: (i // per_batch, 0, 0)),
            pl.BlockSpec((None, D_MODEL, D_FF), lambda i: (layer, 0, 0), pipeline_mode=pl.Buffered(1)),
            pl.BlockSpec((None, D_FF, D_MODEL), lambda i: (layer, 0, 0), pipeline_mode=pl.Buffered(1)),
            pl.BlockSpec((1, D_MODEL), lambda i: (0, 0)),
        ],
        out_specs=pl.BlockSpec((tm, D_MODEL), lambda i: (i, 0)),
        out_shape=jax.ShapeDtypeStruct((TOKENS, D_MODEL), F32),
        compiler_params=pltpu.CompilerParams(
            dimension_semantics=("parallel",), vmem_limit_bytes=VMEM_LIMIT_FUSED),
        name="mlp_final" if final_norm else "mlp",
    )(x, g, sc, sh, gate, up, down, final_w)


def kernel(x, c, ada_w, ada_b, mix_norm_w, mlp_norm_w, mlp_up, mlp_down, ssd_in_w, ssd_conv_w,
           ssd_conv_b, ssd_dt_bias, ssd_A_log, ssd_D, ssd_norm_w, ssd_out_w, sc_in_w, sc_conv_w,
           sc_out_w, final_norm_w):
    xf = x.reshape(TOKENS, D_MODEL)
    mod = _ada_mod(c, ada_w, ada_b).reshape(DEPTH, BATCH, ADA_MODS, 1, D_MODEL)

    def mods(i):
        return [mod[i, :, k] for k in range(ADA_MODS)]

    up_bf = mlp_up.astype(BF16)
    down_bf = mlp_down.astype(BF16)

    sh_m, sc_m, g_m, sh_f, sc_f, g_f = mods(0)
    w_zx = ssd_in_w[0].astype(BF16)
    w_dt = jnp.zeros((D_MODEL, LANES), BF16).at[:, :SSD_N_HEADS].set(
        ssd_in_w[0, :, SSD_ZX_DIM:].astype(BF16))
    dt_bias = jnp.zeros((1, LANES), F32).at[0, :SSD_N_HEADS].set(ssd_dt_bias[0])
    xf = _mixer0(xf, mix_norm_w[0].reshape(1, D_MODEL), sc_m, sh_m, g_m, w_zx, w_dt, dt_bias,
                 ssd_out_w[0].astype(BF16), ssd_conv_w[0], ssd_conv_b[0], ssd_A_log[0], ssd_D[0],
                 ssd_norm_w[0])
    xf = _mlp(xf, mlp_norm_w[0].reshape(1, D_MODEL), sc_f, sh_f, g_f, up_bf, down_bf, 0,
              final_norm_w.reshape(1, D_MODEL), tm=1024, final_norm=False)

    sh_m, sc_m, g_m, sh_f, sc_f, g_f = mods(1)
    xf = _mixer1(xf, mix_norm_w[1].reshape(1, D_MODEL), sc_m, sh_m, g_m, sc_in_w[0].astype(BF16),
                 sc_conv_w[0], sc_out_w[0].astype(BF16), tm=1024)
    xf = _mlp(xf, mlp_norm_w[1].reshape(1, D_MODEL), sc_f, sh_f, g_f, up_bf, down_bf, 1,
              final_norm_w.reshape(1, D_MODEL), tm=1024, final_norm=True)
    return xf.reshape(BATCH, SEQ, D_MODEL)
```

```python
import functools
import math

import jax
import jax.numpy as jnp
from jax import lax
from jax.experimental import pallas as pl
from jax.experimental.pallas import tpu as pltpu

F32 = jnp.float32
BF16 = jnp.bfloat16

D_MODEL = 1024
BATCH = 8
SEQ = 2048
TOKENS = BATCH * SEQ
DEPTH = 2
NORM_EPS = 1e-5
ADA_MODS = 6

SSD_D_INNER = 2048
SSD_HEAD_DIM = 64
SSD_N_HEADS = 32
SSD_N_GROUPS = 4
SSD_D_STATE = 128
SSD_CONV_K = 4
SSD_CHUNK = 128
SSD_BC = SSD_N_GROUPS * SSD_D_STATE
SSD_CONV_DIM = SSD_D_INNER + 2 * SSD_BC
SSD_ZX_DIM = SSD_D_INNER + SSD_CONV_DIM
GROUP_WIDTH = SSD_D_INNER // SSD_N_GROUPS

SC_WIDTH = D_MODEL
SC_CONV_K = 3
D_FF = 4 * D_MODEL

LANES = 128
SUBLANES = 8
VMEM_LIMIT = 48 * 1024 * 1024
VMEM_LIMIT_FUSED = 56 * 1024 * 1024
LOG2E = math.log2(math.e)

MIX_TILE = 256
IN_PROJ_COLS = 1024
SC_SUB_TILE = 256
MLP_FF_CHUNK = 1024


def _silu(v):
    return v * jax.nn.sigmoid(v)


def _softplus(v):
    return jnp.maximum(v, 0.0) + jnp.log1p(jnp.exp(-jnp.abs(v)))


def _norm_mod(x, g, sc, sh):
    ms = jnp.mean(x * x, axis=-1, keepdims=True)
    y = x * lax.rsqrt(ms + NORM_EPS)
    return (y * g) * (1.0 + sc) + sh


def _delay1(t):
    r = pltpu.roll(t, 1, 1)
    sub = lax.broadcasted_iota(jnp.int32, t.shape, 1)
    prev_blk = jnp.concatenate([r[:1], r[:-1]], axis=0)
    return jnp.where(sub < 1, prev_blk, r)


def _causal_conv(cur, prev8, w_ref):
    rows, width = cur.shape
    nblk = rows // SUBLANES
    ext = jnp.concatenate([prev8, cur], axis=0).reshape(nblk + 1, SUBLANES, width)
    acc = ext * w_ref[0:1, :]
    for k in range(1, w_ref.shape[0]):
        acc = _delay1(acc) + ext * w_ref[k:k + 1, :]
    return acc[1:].reshape(rows, width)


def _split_bf16(v, terms):
    parts = []
    r = v
    for t in range(terms):
        p = r.astype(BF16)
        parts.append(p)
        if t + 1 < terms:
            r = r - p.astype(F32)
    return jnp.concatenate(parts, axis=1)


def _ada_kernel(c_ref, w_ref, b_ref, o_ref):
    cond = _silu(c_ref[...])
    o_ref[0] = jnp.dot(cond.astype(BF16), w_ref[0].astype(BF16),
                       preferred_element_type=F32) + b_ref[0]


def _ada_mod(c, ada_w, ada_b):
    tn = 1536
    n = ADA_MODS * D_MODEL
    return pl.pallas_call(
        _ada_kernel,
        grid=(DEPTH, n // tn),
        in_specs=[
            pl.BlockSpec((BATCH, D_MODEL), lambda i, j: (0, 0)),
            pl.BlockSpec((1, D_MODEL, tn), lambda i, j: (i, 0, j)),
            pl.BlockSpec((1, 1, tn), lambda i, j: (i, 0, j)),
        ],
        out_specs=pl.BlockSpec((1, BATCH, tn), lambda i, j: (i, 0, j)),
        out_shape=jax.ShapeDtypeStruct((DEPTH, BATCH, n), F32),
        compiler_params=pltpu.CompilerParams(
            dimension_semantics=("parallel", "parallel"), vmem_limit_bytes=VMEM_LIMIT),
        name="ada_mod",
    )(c, ada_w, ada_b.reshape(DEPTH, 1, n))


def _emit_next(tasks):
    if tasks:
        tasks.pop(0)()


def _ssd_chunk(cur, prev8, z_of, dt, cw_ref, cb_ref, alog_ref, dskip_ref, nw_ref, sel_ref, tri_ref,
               state_ref, out_store, side):
    xbc = _silu(_causal_conv(cur, prev8, cw_ref) + cb_ref[...])
    _emit_next(side)
    xs = xbc[:, :SSD_D_INNER]
    bm = xbc[:, SSD_D_INNER:SSD_D_INNER + SSD_BC]
    cm = xbc[:, SSD_D_INNER + SSD_BC:]

    a2 = dt * (-jnp.exp(alog_ref[...]) * LOG2E)
    a2_parts = _split_bf16(a2, 3)
    acs = jnp.dot(tri_ref[...],
                  jnp.concatenate([a2_parts[:, k * LANES:(k + 1) * LANES] for k in range(3)], axis=0),
                  preferred_element_type=F32)
    last = acs[SSD_CHUNK - 1:SSD_CHUNK, :]
    src_t = (acs - jnp.log2(dt)).T
    out_scale_c = jnp.exp2(acs)
    w_state_c = dt * jnp.exp2(last - acs)
    expanded = jnp.dot(jnp.concatenate([_split_bf16(w_state_c, 2), _split_bf16(out_scale_c, 2)], axis=0),
                       sel_ref[...], preferred_element_type=F32)
    w_state = expanded[:SSD_CHUNK]
    out_scale = expanded[SSD_CHUNK:]
    chunk_decay = out_scale[SSD_CHUNK - 1:SSD_CHUNK, :]
    x_state = (xs * w_state).astype(BF16)

    li = lax.broadcasted_iota(jnp.int32, (SSD_CHUNK, SSD_CHUNK), 0)
    si = lax.broadcasted_iota(jnp.int32, (SSD_CHUNK, SSD_CHUNK), 1)
    causal = li >= si
    left_half = si < SSD_HEAD_DIM

    for g in range(SSD_N_GROUPS):
        if g % 2 == 1:
            _emit_next(side)
        gcol = g * GROUP_WIDTH
        b_g = bm[:, g * SSD_D_STATE:(g + 1) * SSD_D_STATE]
        c_g = cm[:, g * SSD_D_STATE:(g + 1) * SSD_D_STATE].astype(BF16)
        scores = lax.dot_general(c_g, b_g.astype(BF16), (((1,), (1,)), ((), ())),
                                 preferred_element_type=F32)
        state = state_ref[g]
        y_off = jnp.dot(c_g, state.astype(BF16), preferred_element_type=F32)
        y_off = y_off * out_scale[:, gcol:gcol + GROUP_WIDTH]
        new_state = jnp.dot(b_g.T.astype(BF16), x_state[:, gcol:gcol + GROUP_WIDTH],
                            preferred_element_type=F32)
        state_ref[g] = state * chunk_decay[:, gcol:gcol + GROUP_WIDTH] + new_state

        y_parts = []
        for pair in range(GROUP_WIDTH // LANES):
            col = gcol + pair * LANES
            h0 = col // SSD_HEAD_DIM
            m = []
            for h in (h0, h0 + 1):
                seg = acs[:, h:h + 1] - src_t[h:h + 1, :]
                m.append((scores * jnp.where(causal, jnp.exp2(seg), 0.0)).astype(BF16))
            lhs = jnp.concatenate(m, axis=1)
            xp = xs[:, col:col + LANES]
            rhs = jnp.concatenate([jnp.where(left_half, xp, 0.0),
                                   jnp.where(left_half, 0.0, xp)], axis=0).astype(BF16)
            y_diag = jnp.dot(lhs, rhs, preferred_element_type=F32)
            y_parts.append(y_diag + y_off[:, pair * LANES:(pair + 1) * LANES]
                           + dskip_ref[:, col:col + LANES] * xp)
        y = jnp.concatenate(y_parts, axis=1)

        yg = y * _silu(z_of(gcol))
        ms = jnp.mean(yg * yg, axis=-1, keepdims=True)
        yn = yg * lax.rsqrt(ms + NORM_EPS) * nw_ref[:, gcol:gcol + GROUP_WIDTH]
        out_store(gcol, yn.astype(BF16))


def _mixer0_kernel(xb_ref, xa1_ref, xa2_ref, g_ref, sc_ref, sh_ref, gate_ref,
                   wzx_ref, wdt_ref, dtbias_ref, wout_ref,
                   cw_ref, cb_ref, alog_ref, dskip_ref, nw_ref, sel_ref, tri_ref, up_ref, down_ref,
                   o_ref, up_bf_ref, down_bf_ref,
                   zx_a, zx_b, dt_a, dt_b, state_ref, tail_ref, yg_ref):
    up_bf_ref[...] = up_ref[...].astype(BF16)
    down_bf_ref[...] = down_ref[...].astype(BF16)
    i = pl.program_id(0)
    tiles_per_batch = SEQ // MIX_TILE
    last_tile = TOKENS // MIX_TILE - 1

    def in_proj_tasks(x_ref, batch, zx_out, dt_out):
        cell = []

        def head():
            h = _norm_mod(x_ref[...], g_ref[...], sc_ref[batch], sh_ref[batch]).astype(BF16)
            cell.append(h)
            raw = jnp.dot(h, wdt_ref[...], preferred_element_type=F32) + dtbias_ref[...]
            dt_out[...] = _softplus(raw)

        def cols(j):
            def task():
                c0 = j * IN_PROJ_COLS
                zx_out[:, c0:c0 + IN_PROJ_COLS] = jnp.dot(
                    cell[0], wzx_ref[:, c0:c0 + IN_PROJ_COLS], preferred_element_type=F32)
            return task

        return [head] + [cols(j) for j in range(SSD_ZX_DIM // IN_PROJ_COLS)]

    def mix(zx_in, dt_in, first, rows, side):
        for k in range(MIX_TILE // SSD_CHUNK):
            r0 = k * SSD_CHUNK
            cur = zx_in[r0:r0 + SSD_CHUNK, SSD_D_INNER:]
            if k == 0:
                prev8 = tail_ref[...] if first is None else jnp.where(first, 0.0, tail_ref[...])
            else:
                prev8 = zx_in[r0 - SUBLANES:r0, SSD_D_INNER:]

            def z_of(gcol, r0=r0):
                return zx_in[r0:r0 + SSD_CHUNK, gcol:gcol + GROUP_WIDTH]

            def out_store(gcol, v, r0=r0):
                yg_ref[r0:r0 + SSD_CHUNK, gcol:gcol + GROUP_WIDTH] = v

            _ssd_chunk(cur, prev8, z_of, dt_in[r0:r0 + SSD_CHUNK, :], cw_ref, cb_ref, alog_ref,
                       dskip_ref, nw_ref, sel_ref, tri_ref, state_ref, out_store, side)
        while side:
            _emit_next(side)
        tail_ref[...] = zx_in[MIX_TILE - SUBLANES:, SSD_D_INNER:]
        y = jnp.dot(yg_ref[...], wout_ref[...], preferred_element_type=F32)
        o_ref[rows, :] = xb_ref[rows, :] + gate_ref[i // (tiles_per_batch // 2)] * y

    @pl.when(i == 0)
    def _():
        tail_ref[...] = jnp.zeros_like(tail_ref)
        for task in in_proj_tasks(xb_ref.at[0:MIX_TILE, :], 0, zx_a, dt_a):
            task()

    first = (i % (tiles_per_batch // 2)) == 0

    @pl.when(first)
    def _():
        state_ref[...] = jnp.zeros_like(state_ref)

    mix(zx_a, dt_a, first, slice(0, MIX_TILE),
        in_proj_tasks(xa1_ref, (2 * i + 1) // tiles_per_batch, zx_b, dt_b))
    mix(zx_b, dt_b, None, slice(MIX_TILE, 2 * MIX_TILE),
        in_proj_tasks(xa2_ref, jnp.minimum(2 * i + 2, last_tile) // tiles_per_batch, zx_a, dt_a))


def _mixer0(x, g, sc, sh, gate, w_zx, w_dt, dt_bias, w_out, conv_w, conv_b, a_log, d_skip, norm_w,
            mlp_up, mlp_down):
    n_tiles = TOKENS // MIX_TILE
    steps = n_tiles // 2
    up_rows = D_MODEL // steps
    down_rows = D_FF // steps
    head_of_lane = jnp.arange(SSD_D_INNER) // SSD_HEAD_DIM
    sel = (jnp.arange(LANES)[:, None] == head_of_lane[None, :]).astype(BF16)
    sel2 = jnp.concatenate([sel, sel], axis=0)
    tri = (jnp.arange(SSD_CHUNK)[:, None] >= jnp.arange(SSD_CHUNK)[None, :]).astype(BF16)
    tri3 = jnp.concatenate([tri, tri, tri], axis=1)
    a_log_pad = jnp.zeros((1, LANES), F32).at[0, :SSD_N_HEADS].set(a_log)
    d_e = jnp.repeat(d_skip, SSD_HEAD_DIM).reshape(1, SSD_D_INNER)

    def resident(shape):
        return pl.BlockSpec(shape, lambda i: (0,) * len(shape), pipeline_mode=pl.Buffered(1))

    return pl.pallas_call(
        _mixer0_kernel,
        grid=(steps,),
        in_specs=[
            pl.BlockSpec((2 * MIX_TILE, D_MODEL), lambda i: (i, 0)),
            pl.BlockSpec((MIX_TILE, D_MODEL), lambda i: (2 * i + 1, 0)),
            pl.BlockSpec((MIX_TILE, D_MODEL), lambda i: (jnp.minimum(2 * i + 2, n_tiles - 1), 0)),
            resident((1, D_MODEL)),
            resident((BATCH, 1, D_MODEL)),
            resident((BATCH, 1, D_MODEL)),
            resident((BATCH, 1, D_MODEL)),
            resident((D_MODEL, SSD_ZX_DIM)),
            resident((D_MODEL, LANES)),
            resident((1, LANES)),
            resident((SSD_D_INNER, D_MODEL)),
            resident((SSD_CONV_K, SSD_CONV_DIM)),
            resident((1, SSD_CONV_DIM)),
            resident((1, LANES)),
            resident((1, SSD_D_INNER)),
            resident((1, SSD_D_INNER)),
            resident((2 * LANES, SSD_D_INNER)),
            resident((SSD_CHUNK, 3 * SSD_CHUNK)),
            pl.BlockSpec((None, up_rows, D_FF), lambda i: (0, i, 0)),
            pl.BlockSpec((None, down_rows, D_MODEL), lambda i: (0, i, 0)),
        ],
        out_specs=[
            pl.BlockSpec((2 * MIX_TILE, D_MODEL), lambda i: (i, 0)),
            pl.BlockSpec((up_rows, D_FF), lambda i: (i, 0)),
            pl.BlockSpec((down_rows, D_MODEL), lambda i: (i, 0)),
        ],
        out_shape=[
            jax.ShapeDtypeStruct((TOKENS, D_MODEL), F32),
            jax.ShapeDtypeStruct((D_MODEL, D_FF), BF16),
            jax.ShapeDtypeStruct((D_FF, D_MODEL), BF16),
        ],
        scratch_shapes=[
            pltpu.VMEM((MIX_TILE, SSD_ZX_DIM), F32),
            pltpu.VMEM((MIX_TILE, SSD_ZX_DIM), F32),
            pltpu.VMEM((MIX_TILE, LANES), F32),
            pltpu.VMEM((MIX_TILE, LANES), F32),
            pltpu.VMEM((SSD_N_GROUPS, SSD_D_STATE, GROUP_WIDTH), F32),
            pltpu.VMEM((SUBLANES, SSD_CONV_DIM), F32),
            pltpu.VMEM((MIX_TILE, SSD_D_INNER), BF16),
        ],
        compiler_params=pltpu.CompilerParams(
            dimension_semantics=("arbitrary",), vmem_limit_bytes=VMEM_LIMIT_FUSED),
        name="ssd_mixer",
    )(x, x, x, g, sc, sh, gate, w_zx, w_dt, dt_bias, w_out, conv_w, conv_b.reshape(1, SSD_CONV_DIM),
      a_log_pad, d_e, norm_w.reshape(1, SSD_D_INNER), sel2, tri3, mlp_up, mlp_down)


def _mixer1_kernel(x_ref, g_ref, sc_ref, sh_ref, gate_ref, win_ref, cw_ref, wout_ref, o_ref, tail_ref,
                   *, per_batch):
    first_tile = (pl.program_id(0) % per_batch) == 0

    @pl.when(pl.program_id(0) == 0)
    def _():
        tail_ref[...] = jnp.zeros_like(tail_ref)

    u_tail = jnp.where(first_tile, 0.0, tail_ref[...])
    rows = x_ref.shape[0]
    for r0 in range(0, rows, SC_SUB_TILE):
        x = x_ref[r0:r0 + SC_SUB_TILE, :]
        h = _norm_mod(x, g_ref[...], sc_ref[0], sh_ref[0]).astype(BF16)
        proj = jnp.dot(h, win_ref[...], preferred_element_type=F32)
        u = proj[:, SC_WIDTH:2 * SC_WIDTH] * proj[:, 2 * SC_WIDTH:]
        y = (proj[:, :SC_WIDTH] * _causal_conv(u, u_tail, cw_ref)).astype(BF16)
        u_tail = u[SC_SUB_TILE - SUBLANES:, :]
        o_ref[r0:r0 + SC_SUB_TILE, :] = x + gate_ref[0] * jnp.dot(
            y, wout_ref[...], preferred_element_type=F32)
    tail_ref[...] = u_tail


def _mixer1(x, g, sc, sh, gate, w_in, conv_w, w_out, tm):
    per_batch = SEQ // tm
    return pl.pallas_call(
        functools.partial(_mixer1_kernel, per_batch=per_batch),
        grid=(TOKENS // tm,),
        in_specs=[
            pl.BlockSpec((tm, D_MODEL), lambda i: (i, 0)),
            pl.BlockSpec((1, D_MODEL), lambda i: (0, 0)),
            pl.BlockSpec((1, 1, D_MODEL), lambda i: (i // per_batch, 0, 0)),
            pl.BlockSpec((1, 1, D_MODEL), lambda i: (i // per_batch, 0, 0)),
            pl.BlockSpec((1, 1, D_MODEL), lambda i: (i // per_batch, 0, 0)),
            pl.BlockSpec((D_MODEL, 3 * SC_WIDTH), lambda i: (0, 0)),
            pl.BlockSpec((SC_CONV_K, SC_WIDTH), lambda i: (0, 0)),
            pl.BlockSpec((SC_WIDTH, D_MODEL), lambda i: (0, 0)),
        ],
        out_specs=pl.BlockSpec((tm, D_MODEL), lambda i: (i, 0)),
        out_shape=jax.ShapeDtypeStruct((TOKENS, D_MODEL), F32),
        scratch_shapes=[pltpu.VMEM((SUBLANES, SC_WIDTH), F32)],
        compiler_params=pltpu.CompilerParams(
            dimension_semantics=("arbitrary",), vmem_limit_bytes=VMEM_LIMIT),
        name="sc_mixer",
    )(x, g, sc, sh, gate, w_in, conv_w, w_out)


def _mlp_kernel(x_ref, g_ref, sc_ref, sh_ref, gate_ref, up_ref, down_ref, fw_ref, *rest, final_norm):
    n_casts = len(rest) // 2
    o_ref = rest[n_casts]
    for src, dst in zip(rest[:n_casts], rest[n_casts + 1:]):
        dst[...] = src[...].astype(BF16)
    h = _norm_mod(x_ref[...], g_ref[...], sc_ref[0], sh_ref[0]).astype(BF16)
    acc = None
    for c0 in range(0, D_FF, MLP_FF_CHUNK):
        a = jnp.maximum(jnp.dot(h, up_ref[:, c0:c0 + MLP_FF_CHUNK], preferred_element_type=F32), 0.0)
        part = jnp.dot((a * a).astype(BF16), down_ref[c0:c0 + MLP_FF_CHUNK, :],
                       preferred_element_type=F32)
        acc = part if acc is None else acc + part
    out = x_ref[...] + gate_ref[0] * acc
    if final_norm:
        ms = jnp.mean(out * out, axis=-1, keepdims=True)
        out = out * lax.rsqrt(ms + NORM_EPS) * fw_ref[...]
    o_ref[...] = out


def _mlp(x, g, sc, sh, gate, up, down, final_w, tm, final_norm, casts=()):
    per_batch = SEQ // tm
    steps = TOKENS // tm
    cast_in_specs, cast_out_specs, cast_shapes = [], [], []
    for w, layer in casts:
        _, rows, cols = w.shape
        cast_in_specs.append(pl.BlockSpec((None, rows // steps, cols), lambda i, layer=layer: (layer, i, 0)))
        cast_out_specs.append(pl.BlockSpec((rows // steps, cols), lambda i: (i, 0)))
        cast_shapes.append(jax.ShapeDtypeStruct((rows, cols), BF16))
    return pl.pallas_call(
        functools.partial(_mlp_kernel, final_norm=final_norm),
        grid=(steps,),
        in_specs=[
            pl.BlockSpec((tm, D_MODEL), lambda i: (i, 0)),
            pl.BlockSpec((1, D_MODEL), lambda i: (0, 0)),
            pl.BlockSpec((1, 1, D_MODEL), lambda i: (i // per_batch, 0, 0)),
            pl.BlockSpec((1, 1, D_MODEL), lambda i: (i // per_batch, 0, 0)),
            pl.BlockSpec((1, 1, D_MODEL), lambda i: (i // per_batch, 0, 0)),
            pl.BlockSpec((D_MODEL, D_FF), lambda i: (0, 0), pipeline_mode=pl.Buffered(1)),
            pl.BlockSpec((D_FF, D_MODEL), lambda i: (0, 0), pipeline_mode=pl.Buffered(1)),
            pl.BlockSpec((1, D_MODEL), lambda i: (0, 0)),
        ] + cast_in_specs,
        out_specs=[pl.BlockSpec((tm, D_MODEL), lambda i: (i, 0))] + cast_out_specs,
        out_shape=[jax.ShapeDtypeStruct((TOKENS, D_MODEL), F32)] + cast_shapes,
        compiler_params=pltpu.CompilerParams(
            dimension_semantics=("arbitrary",), vmem_limit_bytes=VMEM_LIMIT_FUSED),
        name="mlp_final" if final_norm else "mlp",
    )(x, g, sc, sh, gate, up, down, final_w, *[w for w, _ in casts])


def kernel(x, c, ada_w, ada_b, mix_norm_w, mlp_norm_w, mlp_up, mlp_down, ssd_in_w, ssd_conv_w,
           ssd_conv_b, ssd_dt_bias, ssd_A_log, ssd_D, ssd_norm_w, ssd_out_w, sc_in_w, sc_conv_w,
           sc_out_w, final_norm_w):
    xf = x.reshape(TOKENS, D_MODEL)
    mod = _ada_mod(c, ada_w, ada_b).reshape(DEPTH, BATCH, ADA_MODS, 1, D_MODEL)

    def mods(i):
        return [mod[i, :, k] for k in range(ADA_MODS)]

    sh_m, sc_m, g_m, sh_f, sc_f, g_f = mods(0)
    w_zx = ssd_in_w[0].astype(BF16)
    w_dt = jnp.zeros((D_MODEL, LANES), BF16).at[:, :SSD_N_HEADS].set(
        ssd_in_w[0, :, SSD_ZX_DIM:].astype(BF16))
    dt_bias = jnp.zeros((1, LANES), F32).at[0, :SSD_N_HEADS].set(ssd_dt_bias[0])
    xf, up0, down0 = _mixer0(xf, mix_norm_w[0].reshape(1, D_MODEL), sc_m, sh_m, g_m, w_zx, w_dt, dt_bias,
                             ssd_out_w[0].astype(BF16), ssd_conv_w[0], ssd_conv_b[0], ssd_A_log[0],
                             ssd_D[0], ssd_norm_w[0], mlp_up, mlp_down)
    xf, up1, down1, sc_in_bf, sc_out_bf = _mlp(
        xf, mlp_norm_w[0].reshape(1, D_MODEL), sc_f, sh_f, g_f, up0, down0,
        final_norm_w.reshape(1, D_MODEL), tm=1024, final_norm=False,
        casts=[(mlp_up, 1), (mlp_down, 1), (sc_in_w, 0), (sc_out_w, 0)])

    sh_m, sc_m, g_m, sh_f, sc_f, g_f = mods(1)
    xf = _mixer1(xf, mix_norm_w[1].reshape(1, D_MODEL), sc_m, sh_m, g_m, sc_in_bf, sc_conv_w[0],
                 sc_out_bf, tm=1024)
    xf, = _mlp(xf, mlp_norm_w[1].reshape(1, D_MODEL), sc_f, sh_f, g_f, up1, down1,
               final_norm_w.reshape(1, D_MODEL), tm=1024, final_norm=True)
    return xf.reshape(BATCH, SEQ, D_MODEL)
```

```python
import functools
import math

import jax
import jax.numpy as jnp
from jax import lax
from jax.experimental import pallas as pl
from jax.experimental.pallas import tpu as pltpu

F32 = jnp.float32
BF16 = jnp.bfloat16

D_MODEL = 1024
BATCH = 8
SEQ = 2048
TOKENS = BATCH * SEQ
DEPTH = 2
NORM_EPS = 1e-5
ADA_MODS = 6

SSD_D_INNER = 2048
SSD_HEAD_DIM = 64
SSD_N_HEADS = 32
SSD_N_GROUPS = 4
SSD_D_STATE = 128
SSD_CONV_K = 4
SSD_CHUNK = 128
SSD_BC = SSD_N_GROUPS * SSD_D_STATE
SSD_CONV_DIM = SSD_D_INNER + 2 * SSD_BC
SSD_ZX_DIM = SSD_D_INNER + SSD_CONV_DIM
GROUP_WIDTH = SSD_D_INNER // SSD_N_GROUPS

SC_WIDTH = D_MODEL
SC_CONV_K = 3
D_FF = 4 * D_MODEL

LANES = 128
SUBLANES = 8
VMEM_LIMIT = 48 * 1024 * 1024
VMEM_LIMIT_FUSED = 56 * 1024 * 1024
LOG2E = math.log2(math.e)

MIX_TILE = 256
IN_PROJ_COLS = 1024
SC_SUB_TILE = 256
MLP_FF_CHUNK = 1024


def _silu(v):
    return v * jax.nn.sigmoid(v)


def _softplus(v):
    return jnp.maximum(v, 0.0) + jnp.log1p(jnp.exp(-jnp.abs(v)))


def _norm_mod(x, g, sc, sh):
    ms = jnp.mean(x * x, axis=-1, keepdims=True)
    y = x * lax.rsqrt(ms + NORM_EPS)
    return (y * g) * (1.0 + sc) + sh


def _delay1(t):
    r = pltpu.roll(t, 1, 1)
    sub = lax.broadcasted_iota(jnp.int32, t.shape, 1)
    prev_blk = jnp.concatenate([r[:1], r[:-1]], axis=0)
    return jnp.where(sub < 1, prev_blk, r)


def _causal_conv(cur, prev8, w_ref):
    rows, width = cur.shape
    nblk = rows // SUBLANES
    ext = jnp.concatenate([prev8, cur], axis=0).reshape(nblk + 1, SUBLANES, width)
    acc = ext * w_ref[0:1, :]
    for k in range(1, w_ref.shape[0]):
        acc = _delay1(acc) + ext * w_ref[k:k + 1, :]
    return acc[1:].reshape(rows, width)


def _split_bf16(v, terms):
    parts = []
    r = v
    for t in range(terms):
        p = r.astype(BF16)
        parts.append(p)
        if t + 1 < terms:
            r = r - p.astype(F32)
    return jnp.concatenate(parts, axis=1)


def _ada_kernel(c_ref, w_ref, b_ref, win_ref, wout_ref, o_ref, win_bf_ref, wout_bf_ref):
    cond = _silu(c_ref[...])
    o_ref[0] = jnp.dot(cond.astype(BF16), w_ref[0].astype(BF16),
                       preferred_element_type=F32) + b_ref[0]
    win_bf_ref[...] = win_ref[...].astype(BF16)
    wout_bf_ref[...] = wout_ref[...].astype(BF16)


def _ada_mod(c, ada_w, ada_b, ssd_in_w, ssd_out_w):
    tn = 1536
    n = ADA_MODS * D_MODEL
    col_steps = n // tn
    steps = DEPTH * col_steps
    in_rows, in_cols = ssd_in_w.shape[1:]
    out_rows, out_cols = ssd_out_w.shape[1:]
    return pl.pallas_call(
        _ada_kernel,
        grid=(DEPTH, col_steps),
        in_specs=[
            pl.BlockSpec((BATCH, D_MODEL), lambda i, j: (0, 0)),
            pl.BlockSpec((1, D_MODEL, tn), lambda i, j: (i, 0, j)),
            pl.BlockSpec((1, 1, tn), lambda i, j: (i, 0, j)),
            pl.BlockSpec((None, in_rows // steps, in_cols), lambda i, j: (0, i * col_steps + j, 0)),
            pl.BlockSpec((None, out_rows // steps, out_cols), lambda i, j: (0, i * col_steps + j, 0)),
        ],
        out_specs=[
            pl.BlockSpec((1, BATCH, tn), lambda i, j: (i, 0, j)),
            pl.BlockSpec((in_rows // steps, in_cols), lambda i, j: (i * col_steps + j, 0)),
            pl.BlockSpec((out_rows // steps, out_cols), lambda i, j: (i * col_steps + j, 0)),
        ],
        out_shape=[
            jax.ShapeDtypeStruct((DEPTH, BATCH, n), F32),
            jax.ShapeDtypeStruct((in_rows, in_cols), BF16),
            jax.ShapeDtypeStruct((out_rows, out_cols), BF16),
        ],
        compiler_params=pltpu.CompilerParams(
            dimension_semantics=("arbitrary", "arbitrary"), vmem_limit_bytes=VMEM_LIMIT),
        name="ada_mod",
    )(c, ada_w, ada_b.reshape(DEPTH, 1, n), ssd_in_w, ssd_out_w)


def _emit_next(tasks):
    if tasks:
        tasks.pop(0)()


def _ssd_chunk(cur, prev8, z_of, dt, cw_ref, cb_ref, alog_ref, dskip_ref, nw_ref, tri_ref,
               state_ref, out_store, side):
    xbc = _silu(_causal_conv(cur, prev8, cw_ref) + cb_ref[...])
    _emit_next(side)
    xs = xbc[:, :SSD_D_INNER]
    bm = xbc[:, SSD_D_INNER:SSD_D_INNER + SSD_BC]
    cm = xbc[:, SSD_D_INNER + SSD_BC:]

    a2 = dt * (-jnp.exp(alog_ref[...]) * LOG2E)
    a2_parts = _split_bf16(a2, 3)
    acs = jnp.dot(tri_ref[...],
                  jnp.concatenate([a2_parts[:, k * LANES:(k + 1) * LANES] for k in range(3)], axis=0),
                  preferred_element_type=F32)
    last = acs[SSD_CHUNK - 1:SSD_CHUNK, :]
    src_t = (acs - jnp.log2(dt)).T
    w_state_c = dt * jnp.exp2(last - acs)

    li = lax.broadcasted_iota(jnp.int32, (SSD_CHUNK, SSD_CHUNK), 0)
    si = lax.broadcasted_iota(jnp.int32, (SSD_CHUNK, SSD_CHUNK), 1)
    causal = li >= si
    left_half = si < SSD_HEAD_DIM

    for g in range(SSD_N_GROUPS):
        if g % 2 == 1:
            _emit_next(side)
        gcol = g * GROUP_WIDTH
        b_g = bm[:, g * SSD_D_STATE:(g + 1) * SSD_D_STATE]
        c_g = cm[:, g * SSD_D_STATE:(g + 1) * SSD_D_STATE].astype(BF16)
        scores = lax.dot_general(c_g, b_g.astype(BF16), (((1,), (1,)), ((), ())),
                                 preferred_element_type=F32)

        lhs_pairs, out_scale_pairs, w_state_pairs = [], [], []
        for pair in range(GROUP_WIDTH // LANES):
            h0 = (gcol + pair * LANES) // SSD_HEAD_DIM
            m, scale, wst = [], [], []
            for h in (h0, h0 + 1):
                acs_h = jnp.broadcast_to(acs[:, h:h + 1], (SSD_CHUNK, LANES))
                seg = acs_h - src_t[h:h + 1, :]
                m.append((scores * jnp.where(causal, jnp.exp2(seg), 0.0)).astype(BF16))
                scale.append(jnp.exp2(acs_h))
                wst.append(jnp.broadcast_to(w_state_c[:, h:h + 1], (SSD_CHUNK, LANES)))
            lhs_pairs.append(jnp.concatenate(m, axis=1))
            out_scale_pairs.append(jnp.where(left_half, scale[0], scale[1]))
            w_state_pairs.append(jnp.where(left_half, wst[0], wst[1]))
        out_scale = jnp.concatenate(out_scale_pairs, axis=1)
        w_state = jnp.concatenate(w_state_pairs, axis=1)
        chunk_decay = out_scale[SSD_CHUNK - 1:SSD_CHUNK, :]

        state = state_ref[g]
        y_off = jnp.dot(c_g, state.astype(BF16), preferred_element_type=F32) * out_scale
        new_state = jnp.dot(b_g.T.astype(BF16), (xs[:, gcol:gcol + GROUP_WIDTH] * w_state).astype(BF16),
                            preferred_element_type=F32)
        state_ref[g] = state * chunk_decay + new_state

        y_parts = []
        for pair in range(GROUP_WIDTH // LANES):
            col = gcol + pair * LANES
            xp = xs[:, col:col + LANES]
            rhs = jnp.concatenate([jnp.where(left_half, xp, 0.0),
                                   jnp.where(left_half, 0.0, xp)], axis=0).astype(BF16)
            y_diag = jnp.dot(lhs_pairs[pair], rhs, preferred_element_type=F32)
            y_parts.append(y_diag + y_off[:, pair * LANES:(pair + 1) * LANES]
                           + dskip_ref[:, col:col + LANES] * xp)
        y = jnp.concatenate(y_parts, axis=1)

        yg = y * _silu(z_of(gcol))
        ms = jnp.mean(yg * yg, axis=-1, keepdims=True)
        yn = yg * lax.rsqrt(ms + NORM_EPS) * nw_ref[:, gcol:gcol + GROUP_WIDTH]
        out_store(gcol, yn.astype(BF16))


def _mixer0_kernel(xb_ref, xa1_ref, xa2_ref, g_ref, sc_ref, sh_ref, gate_ref,
                   wzx_ref, wdt_ref, dtbias_ref, wout_ref,
                   cw_ref, cb_ref, alog_ref, dskip_ref, nw_ref, tri_ref, up_ref, down_ref,
                   o_ref, up_bf_ref, down_bf_ref,
                   zx_a, zx_b, dt_a, dt_b, state_ref, tail_ref, yg_ref):
    up_bf_ref[...] = up_ref[...].astype(BF16)
    down_bf_ref[...] = down_ref[...].astype(BF16)
    i = pl.program_id(0)
    tiles_per_batch = SEQ // MIX_TILE
    last_tile = TOKENS // MIX_TILE - 1

    def in_proj_tasks(x_ref, batch, zx_out, dt_out):
        cell = []

        def head():
            h = _norm_mod(x_ref[...], g_ref[...], sc_ref[batch], sh_ref[batch]).astype(BF16)
            cell.append(h)
            raw = jnp.dot(h, wdt_ref[...], preferred_element_type=F32) + dtbias_ref[...]
            dt_out[...] = _softplus(raw)

        def cols(j):
            def task():
                c0 = j * IN_PROJ_COLS
                zx_out[:, c0:c0 + IN_PROJ_COLS] = jnp.dot(
                    cell[0], wzx_ref[:, c0:c0 + IN_PROJ_COLS], preferred_element_type=F32)
            return task

        return [head] + [cols(j) for j in range(SSD_ZX_DIM // IN_PROJ_COLS)]

    def mix(zx_in, dt_in, first, rows, side):
        for k in range(MIX_TILE // SSD_CHUNK):
            r0 = k * SSD_CHUNK
            cur = zx_in[r0:r0 + SSD_CHUNK, SSD_D_INNER:]
            if k == 0:
                prev8 = tail_ref[...] if first is None else jnp.where(first, 0.0, tail_ref[...])
            else:
                prev8 = zx_in[r0 - SUBLANES:r0, SSD_D_INNER:]

            def z_of(gcol, r0=r0):
                return zx_in[r0:r0 + SSD_CHUNK, gcol:gcol + GROUP_WIDTH]

            def out_store(gcol, v, r0=r0):
                yg_ref[r0:r0 + SSD_CHUNK, gcol:gcol + GROUP_WIDTH] = v

            _ssd_chunk(cur, prev8, z_of, dt_in[r0:r0 + SSD_CHUNK, :], cw_ref, cb_ref, alog_ref,
                       dskip_ref, nw_ref, tri_ref, state_ref, out_store, side)
        while side:
            _emit_next(side)
        tail_ref[...] = zx_in[MIX_TILE - SUBLANES:, SSD_D_INNER:]
        y = jnp.dot(yg_ref[...], wout_ref[...], preferred_element_type=F32)
        o_ref[rows, :] = xb_ref[rows, :] + gate_ref[i // (tiles_per_batch // 2)] * y

    @pl.when(i == 0)
    def _():
        tail_ref[...] = jnp.zeros_like(tail_ref)
        for task in in_proj_tasks(xb_ref.at[0:MIX_TILE, :], 0, zx_a, dt_a):
            task()

    first = (i % (tiles_per_batch // 2)) == 0

    @pl.when(first)
    def _():
        state_ref[...] = jnp.zeros_like(state_ref)

    mix(zx_a, dt_a, first, slice(0, MIX_TILE),
        in_proj_tasks(xa1_ref, (2 * i + 1) // tiles_per_batch, zx_b, dt_b))
    mix(zx_b, dt_b, None, slice(MIX_TILE, 2 * MIX_TILE),
        in_proj_tasks(xa2_ref, jnp.minimum(2 * i + 2, last_tile) // tiles_per_batch, zx_a, dt_a))


def _mixer0(x, g, sc, sh, gate, w_zx, w_dt, dt_bias, w_out, conv_w, conv_b, a_log, d_skip, norm_w,
            mlp_up, mlp_down):
    n_tiles = TOKENS // MIX_TILE
    steps = n_tiles // 2
    up_rows = D_MODEL // steps
    down_rows = D_FF // steps
    tri = (jnp.arange(SSD_CHUNK)[:, None] >= jnp.arange(SSD_CHUNK)[None, :]).astype(BF16)
    tri3 = jnp.concatenate([tri, tri, tri], axis=1)
    a_log_pad = jnp.zeros((1, LANES), F32).at[0, :SSD_N_HEADS].set(a_log)
    d_e = jnp.repeat(d_skip, SSD_HEAD_DIM).reshape(1, SSD_D_INNER)

    def resident(shape):
        return pl.BlockSpec(shape, lambda i: (0,) * len(shape), pipeline_mode=pl.Buffered(1))

    return pl.pallas_call(
        _mixer0_kernel,
        grid=(steps,),
        in_specs=[
            pl.BlockSpec((2 * MIX_TILE, D_MODEL), lambda i: (i, 0)),
            pl.BlockSpec((MIX_TILE, D_MODEL), lambda i: (2 * i + 1, 0)),
            pl.BlockSpec((MIX_TILE, D_MODEL), lambda i: (jnp.minimum(2 * i + 2, n_tiles - 1), 0)),
            resident((1, D_MODEL)),
            resident((BATCH, 1, D_MODEL)),
            resident((BATCH, 1, D_MODEL)),
            resident((BATCH, 1, D_MODEL)),
            resident((D_MODEL, SSD_ZX_DIM)),
            resident((D_MODEL, LANES)),
            resident((1, LANES)),
            resident((SSD_D_INNER, D_MODEL)),
            resident((SSD_CONV_K, SSD_CONV_DIM)),
            resident((1, SSD_CONV_DIM)),
            resident((1, LANES)),
            resident((1, SSD_D_INNER)),
            resident((1, SSD_D_INNER)),
            resident((SSD_CHUNK, 3 * SSD_CHUNK)),
            pl.BlockSpec((None, up_rows, D_FF), lambda i: (0, i, 0)),
            pl.BlockSpec((None, down_rows, D_MODEL), lambda i: (0, i, 0)),
        ],
        out_specs=[
            pl.BlockSpec((2 * MIX_TILE, D_MODEL), lambda i: (i, 0)),
            pl.BlockSpec((up_rows, D_FF), lambda i: (i, 0)),
            pl.BlockSpec((down_rows, D_MODEL), lambda i: (i, 0)),
        ],
        out_shape=[
            jax.ShapeDtypeStruct((TOKENS, D_MODEL), F32),
            jax.ShapeDtypeStruct((D_MODEL, D_FF), BF16),
            jax.ShapeDtypeStruct((D_FF, D_MODEL), BF16),
        ],
        scratch_shapes=[
            pltpu.VMEM((MIX_TILE, SSD_ZX_DIM), F32),
            pltpu.VMEM((MIX_TILE, SSD_ZX_DIM), F32),
            pltpu.VMEM((MIX_TILE, LANES), F32),
            pltpu.VMEM((MIX_TILE, LANES), F32),
            pltpu.VMEM((SSD_N_GROUPS, SSD_D_STATE, GROUP_WIDTH), F32),
            pltpu.VMEM((SUBLANES, SSD_CONV_DIM), F32),
            pltpu.VMEM((MIX_TILE, SSD_D_INNER), BF16),
        ],
        compiler_params=pltpu.CompilerParams(
            dimension_semantics=("arbitrary",), vmem_limit_bytes=VMEM_LIMIT_FUSED),
        name="ssd_mixer",
    )(x, x, x, g, sc, sh, gate, w_zx, w_dt, dt_bias, w_out, conv_w, conv_b.reshape(1, SSD_CONV_DIM),
      a_log_pad, d_e, norm_w.reshape(1, SSD_D_INNER), tri3, mlp_up, mlp_down)


def _mixer1_kernel(x_ref, g_ref, sc_ref, sh_ref, gate_ref, win_ref, cw_ref, wout_ref, o_ref, tail_ref,
                   *, per_batch):
    first_tile = (pl.program_id(0) % per_batch) == 0

    @pl.when(pl.program_id(0) == 0)
    def _():
        tail_ref[...] = jnp.zeros_like(tail_ref)

    u_tail = jnp.where(first_tile, 0.0, tail_ref[...])
    rows = x_ref.shape[0]
    for r0 in range(0, rows, SC_SUB_TILE):
        x = x_ref[r0:r0 + SC_SUB_TILE, :]
        h = _norm_mod(x, g_ref[...], sc_ref[0], sh_ref[0]).astype(BF16)
        proj = jnp.dot(h, win_ref[...], preferred_element_type=F32)
        u = proj[:, SC_WIDTH:2 * SC_WIDTH] * proj[:, 2 * SC_WIDTH:]
        y = (proj[:, :SC_WIDTH] * _causal_conv(u, u_tail, cw_ref)).astype(BF16)
        u_tail = u[SC_SUB_TILE - SUBLANES:, :]
        o_ref[r0:r0 + SC_SUB_TILE, :] = x + gate_ref[0] * jnp.dot(
            y, wout_ref[...], preferred_element_type=F32)
    tail_ref[...] = u_tail


def _mixer1(x, g, sc, sh, gate, w_in, conv_w, w_out, tm):
    per_batch = SEQ // tm
    return pl.pallas_call(
        functools.partial(_mixer1_kernel, per_batch=per_batch),
        grid=(TOKENS // tm,),
        in_specs=[
            pl.BlockSpec((tm, D_MODEL), lambda i: (i, 0)),
            pl.BlockSpec((1, D_MODEL), lambda i: (0, 0)),
            pl.BlockSpec((1, 1, D_MODEL), lambda i: (i // per_batch, 0, 0)),
            pl.BlockSpec((1, 1, D_MODEL), lambda i: (i // per_batch, 0, 0)),
            pl.BlockSpec((1, 1, D_MODEL), lambda i: (i // per_batch, 0, 0)),
            pl.BlockSpec((D_MODEL, 3 * SC_WIDTH), lambda i: (0, 0)),
            pl.BlockSpec((SC_CONV_K, SC_WIDTH), lambda i: (0, 0)),
            pl.BlockSpec((SC_WIDTH, D_MODEL), lambda i: (0, 0)),
        ],
        out_specs=pl.BlockSpec((tm, D_MODEL), lambda i: (i, 0)),
        out_shape=jax.ShapeDtypeStruct((TOKENS, D_MODEL), F32),
        scratch_shapes=[pltpu.VMEM((SUBLANES, SC_WIDTH), F32)],
        compiler_params=pltpu.CompilerParams(
            dimension_semantics=("arbitrary",), vmem_limit_bytes=VMEM_LIMIT),
        name="sc_mixer",
    )(x, g, sc, sh, gate, w_in, conv_w, w_out)


def _mlp_kernel(x_ref, g_ref, sc_ref, sh_ref, gate_ref, up_ref, down_ref, fw_ref, *rest, final_norm):
    n_casts = len(rest) // 2
    o_ref = rest[n_casts]
    for src, dst in zip(rest[:n_casts], rest[n_casts + 1:]):
        dst[...] = src[...].astype(BF16)
    h = _norm_mod(x_ref[...], g_ref[...], sc_ref[0], sh_ref[0]).astype(BF16)
    acc = None
    for c0 in range(0, D_FF, MLP_FF_CHUNK):
        a = jnp.maximum(jnp.dot(h, up_ref[:, c0:c0 + MLP_FF_CHUNK], preferred_element_type=F32), 0.0)
        part = jnp.dot((a * a).astype(BF16), down_ref[c0:c0 + MLP_FF_CHUNK, :],
                       preferred_element_type=F32)
        acc = part if acc is None else acc + part
    out = x_ref[...] + gate_ref[0] * acc
    if final_norm:
        ms = jnp.mean(out * out, axis=-1, keepdims=True)
        out = out * lax.rsqrt(ms + NORM_EPS) * fw_ref[...]
    o_ref[...] = out


def _mlp(x, g, sc, sh, gate, up, down, final_w, tm, final_norm, casts=()):
    per_batch = SEQ // tm
    steps = TOKENS // tm
    cast_in_specs, cast_out_specs, cast_shapes = [], [], []
    for w, layer in casts:
        _, rows, cols = w.shape
        cast_in_specs.append(pl.BlockSpec((None, rows // steps, cols), lambda i, layer=layer: (layer, i, 0)))
        cast_out_specs.append(pl.BlockSpec((rows // steps, cols), lambda i: (i, 0)))
        cast_shapes.append(jax.ShapeDtypeStruct((rows, cols), BF16))
    return pl.pallas_call(
        functools.partial(_mlp_kernel, final_norm=final_norm),
        grid=(steps,),
        in_specs=[
            pl.BlockSpec((tm, D_MODEL), lambda i: (i, 0)),
            pl.BlockSpec((1, D_MODEL), lambda i: (0, 0)),
            pl.BlockSpec((1, 1, D_MODEL), lambda i: (i // per_batch, 0, 0)),
            pl.BlockSpec((1, 1, D_MODEL), lambda i: (i // per_batch, 0, 0)),
            pl.BlockSpec((1, 1, D_MODEL), lambda i: (i // per_batch, 0, 0)),
            pl.BlockSpec((D_MODEL, D_FF), lambda i: (0, 0), pipeline_mode=pl.Buffered(1)),
            pl.BlockSpec((D_FF, D_MODEL), lambda i: (0, 0), pipeline_mode=pl.Buffered(1)),
            pl.BlockSpec((1, D_MODEL), lambda i: (0, 0)),
        ] + cast_in_specs,
        out_specs=[pl.BlockSpec((tm, D_MODEL), lambda i: (i, 0))] + cast_out_specs,
        out_shape=[jax.ShapeDtypeStruct((TOKENS, D_MODEL), F32)] + cast_shapes,
        compiler_params=pltpu.CompilerParams(
            dimension_semantics=("arbitrary",), vmem_limit_bytes=VMEM_LIMIT_FUSED),
        name="mlp_final" if final_norm else "mlp",
    )(x, g, sc, sh, gate, up, down, final_w, *[w for w, _ in casts])


def kernel(x, c, ada_w, ada_b, mix_norm_w, mlp_norm_w, mlp_up, mlp_down, ssd_in_w, ssd_conv_w,
           ssd_conv_b, ssd_dt_bias, ssd_A_log, ssd_D, ssd_norm_w, ssd_out_w, sc_in_w, sc_conv_w,
           sc_out_w, final_norm_w):
    xf = x.reshape(TOKENS, D_MODEL)
    mod, w_zx, w_out = _ada_mod(c, ada_w, ada_b, ssd_in_w, ssd_out_w)
    mod = mod.reshape(DEPTH, BATCH, ADA_MODS, 1, D_MODEL)

    def mods(i):
        return [mod[i, :, k] for k in range(ADA_MODS)]

    sh_m, sc_m, g_m, sh_f, sc_f, g_f = mods(0)
    w_dt = jnp.zeros((D_MODEL, LANES), BF16).at[:, :SSD_N_HEADS].set(
        ssd_in_w[0, :, SSD_ZX_DIM:].astype(BF16))
    dt_bias = jnp.zeros((1, LANES), F32).at[0, :SSD_N_HEADS].set(ssd_dt_bias[0])
    xf, up0, down0 = _mixer0(xf, mix_norm_w[0].reshape(1, D_MODEL), sc_m, sh_m, g_m, w_zx, w_dt, dt_bias,
                             w_out, ssd_conv_w[0], ssd_conv_b[0], ssd_A_log[0],
                             ssd_D[0], ssd_norm_w[0], mlp_up, mlp_down)
    xf, up1, down1, sc_in_bf, sc_out_bf = _mlp(
        xf, mlp_norm_w[0].reshape(1, D_MODEL), sc_f, sh_f, g_f, up0, down0,
        final_norm_w.reshape(1, D_MODEL), tm=1024, final_norm=False,
        casts=[(mlp_up, 1), (mlp_down, 1), (sc_in_w, 0), (sc_out_w, 0)])

    sh_m, sc_m, g_m, sh_f, sc_f, g_f = mods(1)
    xf = _mixer1(xf, mix_norm_w[1].reshape(1, D_MODEL), sc_m, sh_m, g_m, sc_in_bf, sc_conv_w[0],
                 sc_out_bf, tm=1024)
    xf, = _mlp(xf, mlp_norm_w[1].reshape(1, D_MODEL), sc_f, sh_f, g_f, up1, down1,
               final_norm_w.reshape(1, D_MODEL), tm=1024, final_norm=True)
    return xf.reshape(BATCH, SEQ, D_MODEL)
```

```python
import functools
import math

import jax
import jax.numpy as jnp
from jax import lax
from jax.experimental import pallas as pl
from jax.experimental.pallas import tpu as pltpu

F32 = jnp.float32
BF16 = jnp.bfloat16

D_MODEL = 1024
BATCH = 8
SEQ = 2048
TOKENS = BATCH * SEQ
DEPTH = 2
NORM_EPS = 1e-5
ADA_MODS = 6

SSD_D_INNER = 2048
SSD_HEAD_DIM = 64
SSD_N_HEADS = 32
SSD_N_GROUPS = 4
SSD_D_STATE = 128
SSD_CONV_K = 4
SSD_CHUNK = 128
SSD_BC = SSD_N_GROUPS * SSD_D_STATE
SSD_CONV_DIM = SSD_D_INNER + 2 * SSD_BC
SSD_ZX_DIM = SSD_D_INNER + SSD_CONV_DIM
GROUP_WIDTH = SSD_D_INNER // SSD_N_GROUPS

SC_WIDTH = D_MODEL
SC_CONV_K = 3
D_FF = 4 * D_MODEL

LANES = 128
SUBLANES = 8
VMEM_LIMIT = 48 * 1024 * 1024
VMEM_LIMIT_FUSED = 56 * 1024 * 1024
LOG2E = math.log2(math.e)

MIX_TILE = 256
IN_PROJ_COLS = 1024
SC_SUB_TILE = 256
MLP_FF_CHUNK = 1024


def _silu(v):
    return v * jax.nn.sigmoid(v)


def _softplus(v):
    return jnp.maximum(v, 0.0) + jnp.log1p(jnp.exp(-jnp.abs(v)))


def _norm_mod(x, g, sc, sh):
    ms = jnp.mean(x * x, axis=-1, keepdims=True)
    y = x * lax.rsqrt(ms + NORM_EPS)
    return (y * g) * (1.0 + sc) + sh


def _delay1(t):
    r = pltpu.roll(t, 1, 1)
    sub = lax.broadcasted_iota(jnp.int32, t.shape, 1)
    prev_blk = jnp.concatenate([r[:1], r[:-1]], axis=0)
    return jnp.where(sub < 1, prev_blk, r)


def _causal_conv(cur, prev8, w_ref):
    rows, width = cur.shape
    nblk = rows // SUBLANES
    ext = jnp.concatenate([prev8, cur], axis=0).reshape(nblk + 1, SUBLANES, width)
    acc = ext * w_ref[0:1, :]
    for k in range(1, w_ref.shape[0]):
        acc = _delay1(acc) + ext * w_ref[k:k + 1, :]
    return acc[1:].reshape(rows, width)


def _split_bf16(v, terms):
    parts = []
    r = v
    for t in range(terms):
        p = r.astype(BF16)
        parts.append(p)
        if t + 1 < terms:
            r = r - p.astype(F32)
    return jnp.concatenate(parts, axis=1)


def _ada_kernel(c_ref, w_ref, b_ref, wout_ref, o_ref, wout_bf_ref):
    cond = _silu(c_ref[...])
    o_ref[0] = jnp.dot(cond.astype(BF16), w_ref[0].astype(BF16),
                       preferred_element_type=F32) + b_ref[0]
    wout_bf_ref[...] = wout_ref[...].astype(BF16)


def _ada_mod(c, ada_w, ada_b, ssd_out_w):
    tn = 1536
    n = ADA_MODS * D_MODEL
    col_steps = n // tn
    steps = DEPTH * col_steps
    out_rows, out_cols = ssd_out_w.shape[1:]
    return pl.pallas_call(
        _ada_kernel,
        grid=(DEPTH, col_steps),
        in_specs=[
            pl.BlockSpec((BATCH, D_MODEL), lambda i, j: (0, 0)),
            pl.BlockSpec((1, D_MODEL, tn), lambda i, j: (i, 0, j)),
            pl.BlockSpec((1, 1, tn), lambda i, j: (i, 0, j)),
            pl.BlockSpec((None, out_rows // steps, out_cols), lambda i, j: (0, i * col_steps + j, 0)),
        ],
        out_specs=[
            pl.BlockSpec((1, BATCH, tn), lambda i, j: (i, 0, j)),
            pl.BlockSpec((out_rows // steps, out_cols), lambda i, j: (i * col_steps + j, 0)),
        ],
        out_shape=[
            jax.ShapeDtypeStruct((DEPTH, BATCH, n), F32),
            jax.ShapeDtypeStruct((out_rows, out_cols), BF16),
        ],
        compiler_params=pltpu.CompilerParams(
            dimension_semantics=("arbitrary", "arbitrary"), vmem_limit_bytes=VMEM_LIMIT),
        name="ada_mod",
    )(c, ada_w, ada_b.reshape(DEPTH, 1, n), ssd_out_w)


def _emit_next(tasks):
    if tasks:
        tasks.pop(0)()


def _ssd_chunk(cur, prev8, z_of, dt, cw_ref, cb_ref, alog_ref, dskip_ref, nw_ref, tri_ref,
               state_ref, out_store, side):
    xbc = _silu(_causal_conv(cur, prev8, cw_ref) + cb_ref[...])
    _emit_next(side)
    xs = xbc[:, :SSD_D_INNER]
    bm = xbc[:, SSD_D_INNER:SSD_D_INNER + SSD_BC]
    cm = xbc[:, SSD_D_INNER + SSD_BC:]

    a2 = dt * (-jnp.exp(alog_ref[...]) * LOG2E)
    a2_parts = _split_bf16(a2, 3)
    acs = jnp.dot(tri_ref[...],
                  jnp.concatenate([a2_parts[:, k * LANES:(k + 1) * LANES] for k in range(3)], axis=0),
                  preferred_element_type=F32)
    last = acs[SSD_CHUNK - 1:SSD_CHUNK, :]
    src_t = (acs - jnp.log2(dt)).T
    w_state_c = dt * jnp.exp2(last - acs)

    li = lax.broadcasted_iota(jnp.int32, (SSD_CHUNK, SSD_CHUNK), 0)
    si = lax.broadcasted_iota(jnp.int32, (SSD_CHUNK, SSD_CHUNK), 1)
    causal = li >= si
    left_half = si < SSD_HEAD_DIM

    for g in range(SSD_N_GROUPS):
        if g % 2 == 1:
            _emit_next(side)
        gcol = g * GROUP_WIDTH
        b_g = bm[:, g * SSD_D_STATE:(g + 1) * SSD_D_STATE]
        c_g = cm[:, g * SSD_D_STATE:(g + 1) * SSD_D_STATE].astype(BF16)
        scores = lax.dot_general(c_g, b_g.astype(BF16), (((1,), (1,)), ((), ())),
                                 preferred_element_type=F32)

        lhs_pairs, out_scale_pairs, w_state_pairs = [], [], []
        for pair in range(GROUP_WIDTH // LANES):
            h0 = (gcol + pair * LANES) // SSD_HEAD_DIM
            m, scale, wst = [], [], []
            for h in (h0, h0 + 1):
                acs_h = jnp.broadcast_to(acs[:, h:h + 1], (SSD_CHUNK, LANES))
                seg = acs_h - src_t[h:h + 1, :]
                m.append((scores * jnp.where(causal, jnp.exp2(seg), 0.0)).astype(BF16))
                scale.append(jnp.exp2(acs_h))
                wst.append(jnp.broadcast_to(w_state_c[:, h:h + 1], (SSD_CHUNK, LANES)))
            lhs_pairs.append(jnp.concatenate(m, axis=1))
            out_scale_pairs.append(jnp.where(left_half, scale[0], scale[1]))
            w_state_pairs.append(jnp.where(left_half, wst[0], wst[1]))
        out_scale = jnp.concatenate(out_scale_pairs, axis=1)
        w_state = jnp.concatenate(w_state_pairs, axis=1)
        chunk_decay = out_scale[SSD_CHUNK - 1:SSD_CHUNK, :]

        state = state_ref[g]
        y_off = jnp.dot(c_g, state.astype(BF16), preferred_element_type=F32) * out_scale
        new_state = jnp.dot(b_g.T.astype(BF16), (xs[:, gcol:gcol + GROUP_WIDTH] * w_state).astype(BF16),
                            preferred_element_type=F32)
        state_ref[g] = state * chunk_decay + new_state

        y_parts = []
        for pair in range(GROUP_WIDTH // LANES):
            col = gcol + pair * LANES
            xp = xs[:, col:col + LANES]
            rhs = jnp.concatenate([jnp.where(left_half, xp, 0.0),
                                   jnp.where(left_half, 0.0, xp)], axis=0).astype(BF16)
            y_diag = jnp.dot(lhs_pairs[pair], rhs, preferred_element_type=F32)
            y_parts.append(y_diag + y_off[:, pair * LANES:(pair + 1) * LANES]
                           + dskip_ref[:, col:col + LANES] * xp)
        y = jnp.concatenate(y_parts, axis=1)

        yg = y * _silu(z_of(gcol))
        ms = jnp.mean(yg * yg, axis=-1, keepdims=True)
        yn = yg * lax.rsqrt(ms + NORM_EPS) * nw_ref[:, gcol:gcol + GROUP_WIDTH]
        out_store(gcol, yn.astype(BF16))


def _mixer0_kernel(xb_ref, xa1_ref, xa2_ref, g_ref, sc_ref, sh_ref, gate_ref,
                   wzx_ref, wdt_ref, dtbias_ref, wout_ref,
                   cw_ref, cb_ref, alog_ref, dskip_ref, nw_ref, tri_ref, up_ref, down_ref,
                   o_ref, up_bf_ref, down_bf_ref,
                   zx_a, zx_b, dt_a, dt_b, state_ref, tail_ref, yg_ref):
    up_bf_ref[...] = up_ref[...].astype(BF16)
    down_bf_ref[...] = down_ref[...].astype(BF16)
    i = pl.program_id(0)
    tiles_per_batch = SEQ // MIX_TILE
    last_tile = TOKENS // MIX_TILE - 1

    def in_proj_tasks(x_ref, batch, zx_out, dt_out):
        cell = []

        def head():
            h = _norm_mod(x_ref[...], g_ref[...], sc_ref[batch], sh_ref[batch]).astype(BF16)
            cell.append(h)
            raw = jnp.dot(h, wdt_ref[...], preferred_element_type=F32) + dtbias_ref[...]
            dt_out[...] = _softplus(raw)

        def cols(j):
            def task():
                c0 = j * IN_PROJ_COLS
                zx_out[:, c0:c0 + IN_PROJ_COLS] = jnp.dot(
                    cell[0], wzx_ref[:, c0:c0 + IN_PROJ_COLS], preferred_element_type=F32)
            return task

        return [head] + [cols(j) for j in range(SSD_ZX_DIM // IN_PROJ_COLS)]

    def mix(zx_in, dt_in, first, rows, side):
        for k in range(MIX_TILE // SSD_CHUNK):
            r0 = k * SSD_CHUNK
            cur = zx_in[r0:r0 + SSD_CHUNK, SSD_D_INNER:]
            if k == 0:
                prev8 = tail_ref[...] if first is None else jnp.where(first, 0.0, tail_ref[...])
            else:
                prev8 = zx_in[r0 - SUBLANES:r0, SSD_D_INNER:]

            def z_of(gcol, r0=r0):
                return zx_in[r0:r0 + SSD_CHUNK, gcol:gcol + GROUP_WIDTH]

            def out_store(gcol, v, r0=r0):
                yg_ref[r0:r0 + SSD_CHUNK, gcol:gcol + GROUP_WIDTH] = v

            _ssd_chunk(cur, prev8, z_of, dt_in[r0:r0 + SSD_CHUNK, :], cw_ref, cb_ref, alog_ref,
                       dskip_ref, nw_ref, tri_ref, state_ref, out_store, side)
        while side:
            _emit_next(side)
        tail_ref[...] = zx_in[MIX_TILE - SUBLANES:, SSD_D_INNER:]
        y = jnp.dot(yg_ref[...], wout_ref[...], preferred_element_type=F32)
        o_ref[rows, :] = xb_ref[rows, :] + gate_ref[i // (tiles_per_batch // 2)] * y

    @pl.when(i == 0)
    def _():
        tail_ref[...] = jnp.zeros_like(tail_ref)
        for task in in_proj_tasks(xb_ref.at[0:MIX_TILE, :], 0, zx_a, dt_a):
            task()

    first = (i % (tiles_per_batch // 2)) == 0

    @pl.when(first)
    def _():
        state_ref[...] = jnp.zeros_like(state_ref)

    mix(zx_a, dt_a, first, slice(0, MIX_TILE),
        in_proj_tasks(xa1_ref, (2 * i + 1) // tiles_per_batch, zx_b, dt_b))
    mix(zx_b, dt_b, None, slice(MIX_TILE, 2 * MIX_TILE),
        in_proj_tasks(xa2_ref, jnp.minimum(2 * i + 2, last_tile) // tiles_per_batch, zx_a, dt_a))


def _mixer0(x, g, sc, sh, gate, w_zx, w_dt, dt_bias, w_out, conv_w, conv_b, a_log, d_skip, norm_w,
            mlp_up, mlp_down):
    n_tiles = TOKENS // MIX_TILE
    steps = n_tiles // 2
    up_rows = D_MODEL // steps
    down_rows = D_FF // steps
    tri = (jnp.arange(SSD_CHUNK)[:, None] >= jnp.arange(SSD_CHUNK)[None, :]).astype(BF16)
    tri3 = jnp.concatenate([tri, tri, tri], axis=1)
    a_log_pad = jnp.zeros((1, LANES), F32).at[0, :SSD_N_HEADS].set(a_log)
    d_e = jnp.repeat(d_skip, SSD_HEAD_DIM).reshape(1, SSD_D_INNER)

    def resident(shape):
        return pl.BlockSpec(shape, lambda i: (0,) * len(shape), pipeline_mode=pl.Buffered(1))

    return pl.pallas_call(
        _mixer0_kernel,
        grid=(steps,),
        in_specs=[
            pl.BlockSpec((2 * MIX_TILE, D_MODEL), lambda i: (i, 0)),
            pl.BlockSpec((MIX_TILE, D_MODEL), lambda i: (2 * i + 1, 0)),
            pl.BlockSpec((MIX_TILE, D_MODEL), lambda i: (jnp.minimum(2 * i + 2, n_tiles - 1), 0)),
            resident((1, D_MODEL)),
            resident((BATCH, 1, D_MODEL)),
            resident((BATCH, 1, D_MODEL)),
            resident((BATCH, 1, D_MODEL)),
            resident((D_MODEL, SSD_ZX_DIM)),
            resident((D_MODEL, LANES)),
            resident((1, LANES)),
            resident((SSD_D_INNER, D_MODEL)),
            resident((SSD_CONV_K, SSD_CONV_DIM)),
            resident((1, SSD_CONV_DIM)),
            resident((1, LANES)),
            resident((1, SSD_D_INNER)),
            resident((1, SSD_D_INNER)),
            resident((SSD_CHUNK, 3 * SSD_CHUNK)),
            pl.BlockSpec((None, up_rows, D_FF), lambda i: (0, i, 0)),
            pl.BlockSpec((None, down_rows, D_MODEL), lambda i: (0, i, 0)),
        ],
        out_specs=[
            pl.BlockSpec((2 * MIX_TILE, D_MODEL), lambda i: (i, 0)),
            pl.BlockSpec((up_rows, D_FF), lambda i: (i, 0)),
            pl.BlockSpec((down_rows, D_MODEL), lambda i: (i, 0)),
        ],
        out_shape=[
            jax.ShapeDtypeStruct((TOKENS, D_MODEL), F32),
            jax.ShapeDtypeStruct((D_MODEL, D_FF), BF16),
            jax.ShapeDtypeStruct((D_FF, D_MODEL), BF16),
        ],
        scratch_shapes=[
            pltpu.VMEM((MIX_TILE, SSD_ZX_DIM), F32),
            pltpu.VMEM((MIX_TILE, SSD_ZX_DIM), F32),
            pltpu.VMEM((MIX_TILE, LANES), F32),
            pltpu.VMEM((MIX_TILE, LANES), F32),
            pltpu.VMEM((SSD_N_GROUPS, SSD_D_STATE, GROUP_WIDTH), F32),
            pltpu.VMEM((SUBLANES, SSD_CONV_DIM), F32),
            pltpu.VMEM((MIX_TILE, SSD_D_INNER), BF16),
        ],
        compiler_params=pltpu.CompilerParams(
            dimension_semantics=("arbitrary",), vmem_limit_bytes=VMEM_LIMIT_FUSED),
        name="ssd_mixer",
    )(x, x, x, g, sc, sh, gate, w_zx, w_dt, dt_bias, w_out, conv_w, conv_b.reshape(1, SSD_CONV_DIM),
      a_log_pad, d_e, norm_w.reshape(1, SSD_D_INNER), tri3, mlp_up, mlp_down)


def _mixer1_kernel(x_ref, g_ref, sc_ref, sh_ref, gate_ref, win_ref, cw_ref, wout_ref, o_ref, tail_ref,
                   *, per_batch):
    first_tile = (pl.program_id(0) % per_batch) == 0

    @pl.when(pl.program_id(0) == 0)
    def _():
        tail_ref[...] = jnp.zeros_like(tail_ref)

    u_tail = jnp.where(first_tile, 0.0, tail_ref[...])
    rows = x_ref.shape[0]
    for r0 in range(0, rows, SC_SUB_TILE):
        x = x_ref[r0:r0 + SC_SUB_TILE, :]
        h = _norm_mod(x, g_ref[...], sc_ref[0], sh_ref[0]).astype(BF16)
        proj = jnp.dot(h, win_ref[...], preferred_element_type=F32)
        u = proj[:, SC_WIDTH:2 * SC_WIDTH] * proj[:, 2 * SC_WIDTH:]
        y = (proj[:, :SC_WIDTH] * _causal_conv(u, u_tail, cw_ref)).astype(BF16)
        u_tail = u[SC_SUB_TILE - SUBLANES:, :]
        o_ref[r0:r0 + SC_SUB_TILE, :] = x + gate_ref[0] * jnp.dot(
            y, wout_ref[...], preferred_element_type=F32)
    tail_ref[...] = u_tail


def _mixer1(x, g, sc, sh, gate, w_in, conv_w, w_out, tm):
    per_batch = SEQ // tm
    return pl.pallas_call(
        functools.partial(_mixer1_kernel, per_batch=per_batch),
        grid=(TOKENS // tm,),
        in_specs=[
            pl.BlockSpec((tm, D_MODEL), lambda i: (i, 0)),
            pl.BlockSpec((1, D_MODEL), lambda i: (0, 0)),
            pl.BlockSpec((1, 1, D_MODEL), lambda i: (i // per_batch, 0, 0)),
            pl.BlockSpec((1, 1, D_MODEL), lambda i: (i // per_batch, 0, 0)),
            pl.BlockSpec((1, 1, D_MODEL), lambda i: (i // per_batch, 0, 0)),
            pl.BlockSpec((D_MODEL, 3 * SC_WIDTH), lambda i: (0, 0)),
            pl.BlockSpec((SC_CONV_K, SC_WIDTH), lambda i: (0, 0)),
            pl.BlockSpec((SC_WIDTH, D_MODEL), lambda i: (0, 0)),
        ],
        out_specs=pl.BlockSpec((tm, D_MODEL), lambda i: (i, 0)),
        out_shape=jax.ShapeDtypeStruct((TOKENS, D_MODEL), F32),
        scratch_shapes=[pltpu.VMEM((SUBLANES, SC_WIDTH), F32)],
        compiler_params=pltpu.CompilerParams(
            dimension_semantics=("arbitrary",), vmem_limit_bytes=VMEM_LIMIT),
        name="sc_mixer",
    )(x, g, sc, sh, gate, w_in, conv_w, w_out)


def _mlp_kernel(x_ref, g_ref, sc_ref, sh_ref, gate_ref, up_ref, down_ref, fw_ref, *rest, final_norm):
    n_casts = len(rest) // 2
    o_ref = rest[n_casts]
    for src, dst in zip(rest[:n_casts], rest[n_casts + 1:]):
        dst[...] = src[...].astype(BF16)
    h = _norm_mod(x_ref[...], g_ref[...], sc_ref[0], sh_ref[0]).astype(BF16)
    acc = None
    for c0 in range(0, D_FF, MLP_FF_CHUNK):
        a = jnp.maximum(jnp.dot(h, up_ref[:, c0:c0 + MLP_FF_CHUNK], preferred_element_type=F32), 0.0)
        part = jnp.dot((a * a).astype(BF16), down_ref[c0:c0 + MLP_FF_CHUNK, :],
                       preferred_element_type=F32)
        acc = part if acc is None else acc + part
    out = x_ref[...] + gate_ref[0] * acc
    if final_norm:
        ms = jnp.mean(out * out, axis=-1, keepdims=True)
        out = out * lax.rsqrt(ms + NORM_EPS) * fw_ref[...]
    o_ref[...] = out


def _mlp(x, g, sc, sh, gate, up, down, final_w, tm, final_norm, casts=()):
    per_batch = SEQ // tm
    steps = TOKENS // tm
    cast_in_specs, cast_out_specs, cast_shapes = [], [], []
    for w, layer in casts:
        _, rows, cols = w.shape
        cast_in_specs.append(pl.BlockSpec((None, rows // steps, cols), lambda i, layer=layer: (layer, i, 0)))
        cast_out_specs.append(pl.BlockSpec((rows // steps, cols), lambda i: (i, 0)))
        cast_shapes.append(jax.ShapeDtypeStruct((rows, cols), BF16))
    return pl.pallas_call(
        functools.partial(_mlp_kernel, final_norm=final_norm),
        grid=(steps,),
        in_specs=[
            pl.BlockSpec((tm, D_MODEL), lambda i: (i, 0)),
            pl.BlockSpec((1, D_MODEL), lambda i: (0, 0)),
            pl.BlockSpec((1, 1, D_MODEL), lambda i: (i // per_batch, 0, 0)),
            pl.BlockSpec((1, 1, D_MODEL), lambda i: (i // per_batch, 0, 0)),
            pl.BlockSpec((1, 1, D_MODEL), lambda i: (i // per_batch, 0, 0)),
            pl.BlockSpec((D_MODEL, D_FF), lambda i: (0, 0), pipeline_mode=pl.Buffered(1)),
            pl.BlockSpec((D_FF, D_MODEL), lambda i: (0, 0), pipeline_mode=pl.Buffered(1)),
            pl.BlockSpec((1, D_MODEL), lambda i: (0, 0)),
        ] + cast_in_specs,
        out_specs=[pl.BlockSpec((tm, D_MODEL), lambda i: (i, 0))] + cast_out_specs,
        out_shape=[jax.ShapeDtypeStruct((TOKENS, D_MODEL), F32)] + cast_shapes,
        compiler_params=pltpu.CompilerParams(
            dimension_semantics=("arbitrary",), vmem_limit_bytes=VMEM_LIMIT_FUSED),
        name="mlp_final" if final_norm else "mlp",
    )(x, g, sc, sh, gate, up, down, final_w, *[w for w, _ in casts])


def kernel(x, c, ada_w, ada_b, mix_norm_w, mlp_norm_w, mlp_up, mlp_down, ssd_in_w, ssd_conv_w,
           ssd_conv_b, ssd_dt_bias, ssd_A_log, ssd_D, ssd_norm_w, ssd_out_w, sc_in_w, sc_conv_w,
           sc_out_w, final_norm_w):
    xf = x.reshape(TOKENS, D_MODEL)
    mod, w_out = _ada_mod(c, ada_w, ada_b, ssd_out_w)
    mod = mod.reshape(DEPTH, BATCH, ADA_MODS, 1, D_MODEL)

    def mods(i):
        return [mod[i, :, k] for k in range(ADA_MODS)]

    sh_m, sc_m, g_m, sh_f, sc_f, g_f = mods(0)
    w_zx = ssd_in_w[0].astype(BF16)
    w_dt = jnp.zeros((D_MODEL, LANES), BF16).at[:, :SSD_N_HEADS].set(
        ssd_in_w[0, :, SSD_ZX_DIM:].astype(BF16))
    dt_bias = jnp.zeros((1, LANES), F32).at[0, :SSD_N_HEADS].set(ssd_dt_bias[0])
    xf, up0, down0 = _mixer0(xf, mix_norm_w[0].reshape(1, D_MODEL), sc_m, sh_m, g_m, w_zx, w_dt, dt_bias,
                             w_out, ssd_conv_w[0], ssd_conv_b[0], ssd_A_log[0],
                             ssd_D[0], ssd_norm_w[0], mlp_up, mlp_down)
    xf, up1, down1, sc_in_bf, sc_out_bf = _mlp(
        xf, mlp_norm_w[0].reshape(1, D_MODEL), sc_f, sh_f, g_f, up0, down0,
        final_norm_w.reshape(1, D_MODEL), tm=1024, final_norm=False,
        casts=[(mlp_up, 1), (mlp_down, 1), (sc_in_w, 0), (sc_out_w, 0)])

    sh_m, sc_m, g_m, sh_f, sc_f, g_f = mods(1)
    xf = _mixer1(xf, mix_norm_w[1].reshape(1, D_MODEL), sc_m, sh_m, g_m, sc_in_bf, sc_conv_w[0],
                 sc_out_bf, tm=1024)
    xf, = _mlp(xf, mlp_norm_w[1].reshape(1, D_MODEL), sc_f, sh_f, g_f, up1, down1,
               final_norm_w.reshape(1, D_MODEL), tm=1024, final_norm=True)
    return xf.reshape(BATCH, SEQ, D_MODEL)
```

```python
import functools
import math

import jax
import jax.numpy as jnp
from jax import lax
from jax.experimental import pallas as pl
from jax.experimental.pallas import tpu as pltpu

F32 = jnp.float32
BF16 = jnp.bfloat16

D_MODEL = 1024
BATCH = 8
SEQ = 2048
TOKENS = BATCH * SEQ
DEPTH = 2
NORM_EPS = 1e-5
ADA_MODS = 6

SSD_D_INNER = 2048
SSD_HEAD_DIM = 64
SSD_N_HEADS = 32
SSD_N_GROUPS = 4
SSD_D_STATE = 128
SSD_CONV_K = 4
SSD_CHUNK = 128
SSD_BC = SSD_N_GROUPS * SSD_D_STATE
SSD_CONV_DIM = SSD_D_INNER + 2 * SSD_BC
SSD_ZX_DIM = SSD_D_INNER + SSD_CONV_DIM
GROUP_WIDTH = SSD_D_INNER // SSD_N_GROUPS

SC_WIDTH = D_MODEL
SC_CONV_K = 3
D_FF = 4 * D_MODEL

LANES = 128
SUBLANES = 8
VMEM_LIMIT = 48 * 1024 * 1024
VMEM_LIMIT_FUSED = 56 * 1024 * 1024
LOG2E = math.log2(math.e)

MIX_TILE = 256
IN_PROJ_COLS = 1024
SC_SUB_TILE = 256
MLP_FF_CHUNK = 1024


def _silu(v):
    half = 0.5 * v
    return half * jnp.tanh(half) + half


def _softplus(v):
    return jnp.maximum(v, 0.0) + jnp.log1p(jnp.exp(-jnp.abs(v)))


def _norm_mod(x, g, sc, sh):
    ms = jnp.mean(x * x, axis=-1, keepdims=True)
    y = x * lax.rsqrt(ms + NORM_EPS)
    return (y * g) * (1.0 + sc) + sh


def _delay1(t):
    r = pltpu.roll(t, 1, 1)
    sub = lax.broadcasted_iota(jnp.int32, t.shape, 1)
    prev_blk = jnp.concatenate([r[:1], r[:-1]], axis=0)
    return jnp.where(sub < 1, prev_blk, r)


def _causal_conv(cur, prev8, w_ref):
    rows, width = cur.shape
    nblk = rows // SUBLANES
    ext = jnp.concatenate([prev8, cur], axis=0).reshape(nblk + 1, SUBLANES, width)
    acc = ext * w_ref[0:1, :]
    for k in range(1, w_ref.shape[0]):
        acc = _delay1(acc) + ext * w_ref[k:k + 1, :]
    return acc[1:].reshape(rows, width)


def _split_bf16(v, terms):
    parts = []
    r = v
    for t in range(terms):
        p = r.astype(BF16)
        parts.append(p)
        if t + 1 < terms:
            r = r - p.astype(F32)
    return jnp.concatenate(parts, axis=1)


def _ada_kernel(c_ref, w_ref, b_ref, wout_ref, o_ref, wout_bf_ref):
    cond = _silu(c_ref[...])
    o_ref[0] = jnp.dot(cond.astype(BF16), w_ref[0].astype(BF16),
                       preferred_element_type=F32) + b_ref[0]
    wout_bf_ref[...] = wout_ref[...].astype(BF16)


def _ada_mod(c, ada_w, ada_b, ssd_out_w):
    tn = 1536
    n = ADA_MODS * D_MODEL
    col_steps = n // tn
    steps = DEPTH * col_steps
    out_rows, out_cols = ssd_out_w.shape[1:]
    return pl.pallas_call(
        _ada_kernel,
        grid=(DEPTH, col_steps),
        in_specs=[
            pl.BlockSpec((BATCH, D_MODEL), lambda i, j: (0, 0)),
            pl.BlockSpec((1, D_MODEL, tn), lambda i, j: (i, 0, j)),
            pl.BlockSpec((1, 1, tn), lambda i, j: (i, 0, j)),
            pl.BlockSpec((None, out_rows // steps, out_cols), lambda i, j: (0, i * col_steps + j, 0)),
        ],
        out_specs=[
            pl.BlockSpec((1, BATCH, tn), lambda i, j: (i, 0, j)),
            pl.BlockSpec((out_rows // steps, out_cols), lambda i, j: (i * col_steps + j, 0)),
        ],
        out_shape=[
            jax.ShapeDtypeStruct((DEPTH, BATCH, n), F32),
            jax.ShapeDtypeStruct((out_rows, out_cols), BF16),
        ],
        compiler_params=pltpu.CompilerParams(
            dimension_semantics=("arbitrary", "arbitrary"), vmem_limit_bytes=VMEM_LIMIT),
        name="ada_mod",
    )(c, ada_w, ada_b.reshape(DEPTH, 1, n), ssd_out_w)


def _emit_next(tasks):
    if tasks:
        tasks.pop(0)()


def _ssd_chunk(cur, prev8, z_of, dt, cw_ref, cb_ref, alog_ref, dskip_ref, nw_ref, tri_ref,
               state_ref, out_store, side):
    xbc = _silu(_causal_conv(cur, prev8, cw_ref) + cb_ref[...])
    _emit_next(side)
    xs = xbc[:, :SSD_D_INNER]
    bm = xbc[:, SSD_D_INNER:SSD_D_INNER + SSD_BC]
    cm = xbc[:, SSD_D_INNER + SSD_BC:]

    a2 = dt * (-jnp.exp(alog_ref[...]) * LOG2E)
    a2_parts = _split_bf16(a2, 3)
    acs = jnp.dot(tri_ref[...],
                  jnp.concatenate([a2_parts[:, k * LANES:(k + 1) * LANES] for k in range(3)], axis=0),
                  preferred_element_type=F32)
    last = acs[SSD_CHUNK - 1:SSD_CHUNK, :]
    src_t = (acs - jnp.log2(dt)).T
    w_state_c = dt * jnp.exp2(last - acs)

    li = lax.broadcasted_iota(jnp.int32, (SSD_CHUNK, SSD_CHUNK), 0)
    si = lax.broadcasted_iota(jnp.int32, (SSD_CHUNK, SSD_CHUNK), 1)
    causal = li >= si
    left_half = si < SSD_HEAD_DIM

    for g in range(SSD_N_GROUPS):
        if g % 2 == 1:
            _emit_next(side)
        gcol = g * GROUP_WIDTH
        b_g = bm[:, g * SSD_D_STATE:(g + 1) * SSD_D_STATE]
        c_g = cm[:, g * SSD_D_STATE:(g + 1) * SSD_D_STATE].astype(BF16)
        scores = lax.dot_general(c_g, b_g.astype(BF16), (((1,), (1,)), ((), ())),
                                 preferred_element_type=F32)

        lhs_pairs, out_scale_pairs, w_state_pairs = [], [], []
        for pair in range(GROUP_WIDTH // LANES):
            h0 = (gcol + pair * LANES) // SSD_HEAD_DIM
            m, scale, wst = [], [], []
            for h in (h0, h0 + 1):
                acs_h = jnp.broadcast_to(acs[:, h:h + 1], (SSD_CHUNK, LANES))
                seg = acs_h - src_t[h:h + 1, :]
                m.append((scores * jnp.where(causal, jnp.exp2(seg), 0.0)).astype(BF16))
                scale.append(jnp.exp2(acs_h))
                wst.append(jnp.broadcast_to(w_state_c[:, h:h + 1], (SSD_CHUNK, LANES)))
            lhs_pairs.append(jnp.concatenate(m, axis=1))
            out_scale_pairs.append(jnp.where(left_half, scale[0], scale[1]))
            w_state_pairs.append(jnp.where(left_half, wst[0], wst[1]))
        out_scale = jnp.concatenate(out_scale_pairs, axis=1)
        w_state = jnp.concatenate(w_state_pairs, axis=1)
        chunk_decay = out_scale[SSD_CHUNK - 1:SSD_CHUNK, :]

        state = state_ref[g]
        y_off = jnp.dot(c_g, state.astype(BF16), preferred_element_type=F32) * out_scale
        new_state = jnp.dot(b_g.T.astype(BF16), (xs[:, gcol:gcol + GROUP_WIDTH] * w_state).astype(BF16),
                            preferred_element_type=F32)
        state_ref[g] = state * chunk_decay + new_state

        y_parts = []
        for pair in range(GROUP_WIDTH // LANES):
            col = gcol + pair * LANES
            xp = xs[:, col:col + LANES]
            rhs = jnp.concatenate([jnp.where(left_half, xp, 0.0),
                                   jnp.where(left_half, 0.0, xp)], axis=0).astype(BF16)
            y_diag = jnp.dot(lhs_pairs[pair], rhs, preferred_element_type=F32)
            y_parts.append(y_diag + y_off[:, pair * LANES:(pair + 1) * LANES]
                           + dskip_ref[:, col:col + LANES] * xp)
        y = jnp.concatenate(y_parts, axis=1)

        yg = y * _silu(z_of(gcol))
        ms = jnp.mean(yg * yg, axis=-1, keepdims=True)
        yn = yg * lax.rsqrt(ms + NORM_EPS) * nw_ref[:, gcol:gcol + GROUP_WIDTH]
        out_store(gcol, yn.astype(BF16))


def _mixer0_kernel(xb_ref, xa1_ref, xa2_ref, g_ref, sc_ref, sh_ref, gate_ref,
                   wzx_ref, wdt_ref, dtbias_ref, wout_ref,
                   cw_ref, cb_ref, alog_ref, dskip_ref, nw_ref, tri_ref, up_ref, down_ref,
                   o_ref, up_bf_ref, down_bf_ref,
                   zx_a, zx_b, dt_a, dt_b, state_ref, tail_ref, yg_ref):
    up_bf_ref[...] = up_ref[...].astype(BF16)
    down_bf_ref[...] = down_ref[...].astype(BF16)
    i = pl.program_id(0)
    tiles_per_batch = SEQ // MIX_TILE
    last_tile = TOKENS // MIX_TILE - 1

    def in_proj_tasks(x_ref, batch, zx_out, dt_out):
        cell = []

        def head():
            h = _norm_mod(x_ref[...], g_ref[...], sc_ref[batch], sh_ref[batch]).astype(BF16)
            cell.append(h)
            raw = jnp.dot(h, wdt_ref[...], preferred_element_type=F32) + dtbias_ref[...]
            dt_out[...] = _softplus(raw)

        def cols(j):
            def task():
                c0 = j * IN_PROJ_COLS
                zx_out[:, c0:c0 + IN_PROJ_COLS] = jnp.dot(
                    cell[0], wzx_ref[:, c0:c0 + IN_PROJ_COLS], preferred_element_type=F32)
            return task

        return [head] + [cols(j) for j in range(SSD_ZX_DIM // IN_PROJ_COLS)]

    def mix(zx_in, dt_in, first, rows, side):
        for k in range(MIX_TILE // SSD_CHUNK):
            r0 = k * SSD_CHUNK
            cur = zx_in[r0:r0 + SSD_CHUNK, SSD_D_INNER:]
            if k == 0:
                prev8 = tail_ref[...] if first is None else jnp.where(first, 0.0, tail_ref[...])
            else:
                prev8 = zx_in[r0 - SUBLANES:r0, SSD_D_INNER:]

            def z_of(gcol, r0=r0):
                return zx_in[r0:r0 + SSD_CHUNK, gcol:gcol + GROUP_WIDTH]

            def out_store(gcol, v, r0=r0):
                yg_ref[r0:r0 + SSD_CHUNK, gcol:gcol + GROUP_WIDTH] = v

            _ssd_chunk(cur, prev8, z_of, dt_in[r0:r0 + SSD_CHUNK, :], cw_ref, cb_ref, alog_ref,
                       dskip_ref, nw_ref, tri_ref, state_ref, out_store, side)
        while side:
            _emit_next(side)
        tail_ref[...] = zx_in[MIX_TILE - SUBLANES:, SSD_D_INNER:]
        y = jnp.dot(yg_ref[...], wout_ref[...], preferred_element_type=F32)
        o_ref[rows, :] = xb_ref[rows, :] + gate_ref[i // (tiles_per_batch // 2)] * y

    @pl.when(i == 0)
    def _():
        tail_ref[...] = jnp.zeros_like(tail_ref)
        for task in in_proj_tasks(xb_ref.at[0:MIX_TILE, :], 0, zx_a, dt_a):
            task()

    first = (i % (tiles_per_batch // 2)) == 0

    @pl.when(first)
    def _():
        state_ref[...] = jnp.zeros_like(state_ref)

    mix(zx_a, dt_a, first, slice(0, MIX_TILE),
        in_proj_tasks(xa1_ref, (2 * i + 1) // tiles_per_batch, zx_b, dt_b))
    mix(zx_b, dt_b, None, slice(MIX_TILE, 2 * MIX_TILE),
        in_proj_tasks(xa2_ref, jnp.minimum(2 * i + 2, last_tile) // tiles_per_batch, zx_a, dt_a))


def _mixer0(x, g, sc, sh, gate, w_zx, w_dt, dt_bias, w_out, conv_w, conv_b, a_log, d_skip, norm_w,
            mlp_up, mlp_down):
    n_tiles = TOKENS // MIX_TILE
    steps = n_tiles // 2
    up_rows = D_MODEL // steps
    down_rows = D_FF // steps
    tri = (jnp.arange(SSD_CHUNK)[:, None] >= jnp.arange(SSD_CHUNK)[None, :]).astype(BF16)
    tri3 = jnp.concatenate([tri, tri, tri], axis=1)
    a_log_pad = jnp.zeros((1, LANES), F32).at[0, :SSD_N_HEADS].set(a_log)
    d_e = jnp.repeat(d_skip, SSD_HEAD_DIM).reshape(1, SSD_D_INNER)

    def resident(shape):
        return pl.BlockSpec(shape, lambda i: (0,) * len(shape), pipeline_mode=pl.Buffered(1))

    return pl.pallas_call(
        _mixer0_kernel,
        grid=(steps,),
        in_specs=[
            pl.BlockSpec((2 * MIX_TILE, D_MODEL), lambda i: (i, 0)),
            pl.BlockSpec((MIX_TILE, D_MODEL), lambda i: (2 * i + 1, 0)),
            pl.BlockSpec((MIX_TILE, D_MODEL), lambda i: (jnp.minimum(2 * i + 2, n_tiles - 1), 0)),
            resident((1, D_MODEL)),
            resident((BATCH, 1, D_MODEL)),
            resident((BATCH, 1, D_MODEL)),
            resident((BATCH, 1, D_MODEL)),
            resident((D_MODEL, SSD_ZX_DIM)),
            resident((D_MODEL, LANES)),
            resident((1, LANES)),
            resident((SSD_D_INNER, D_MODEL)),
            resident((SSD_CONV_K, SSD_CONV_DIM)),
            resident((1, SSD_CONV_DIM)),
            resident((1, LANES)),
            resident((1, SSD_D_INNER)),
            resident((1, SSD_D_INNER)),
            resident((SSD_CHUNK, 3 * SSD_CHUNK)),
            pl.BlockSpec((None, up_rows, D_FF), lambda i: (0, i, 0)),
            pl.BlockSpec((None, down_rows, D_MODEL), lambda i: (0, i, 0)),
        ],
        out_specs=[
            pl.BlockSpec((2 * MIX_TILE, D_MODEL), lambda i: (i, 0)),
            pl.BlockSpec((up_rows, D_FF), lambda i: (i, 0)),
            pl.BlockSpec((down_rows, D_MODEL), lambda i: (i, 0)),
        ],
        out_shape=[
            jax.ShapeDtypeStruct((TOKENS, D_MODEL), F32),
            jax.ShapeDtypeStruct((D_MODEL, D_FF), BF16),
            jax.ShapeDtypeStruct((D_FF, D_MODEL), BF16),
        ],
        scratch_shapes=[
            pltpu.VMEM((MIX_TILE, SSD_ZX_DIM), F32),
            pltpu.VMEM((MIX_TILE, SSD_ZX_DIM), F32),
            pltpu.VMEM((MIX_TILE, LANES), F32),
            pltpu.VMEM((MIX_TILE, LANES), F32),
            pltpu.VMEM((SSD_N_GROUPS, SSD_D_STATE, GROUP_WIDTH), F32),
            pltpu.VMEM((SUBLANES, SSD_CONV_DIM), F32),
            pltpu.VMEM((MIX_TILE, SSD_D_INNER), BF16),
        ],
        compiler_params=pltpu.CompilerParams(
            dimension_semantics=("arbitrary",), vmem_limit_bytes=VMEM_LIMIT_FUSED),
        name="ssd_mixer",
    )(x, x, x, g, sc, sh, gate, w_zx, w_dt, dt_bias, w_out, conv_w, conv_b.reshape(1, SSD_CONV_DIM),
      a_log_pad, d_e, norm_w.reshape(1, SSD_D_INNER), tri3, mlp_up, mlp_down)


def _mixer1_kernel(x_ref, g_ref, sc_ref, sh_ref, gate_ref, win_ref, cw_ref, wout_ref, o_ref, tail_ref,
                   *, per_batch):
    first_tile = (pl.program_id(0) % per_batch) == 0

    @pl.when(pl.program_id(0) == 0)
    def _():
        tail_ref[...] = jnp.zeros_like(tail_ref)

    u_tail = jnp.where(first_tile, 0.0, tail_ref[...])
    rows = x_ref.shape[0]
    for r0 in range(0, rows, SC_SUB_TILE):
        x = x_ref[r0:r0 + SC_SUB_TILE, :]
        h = _norm_mod(x, g_ref[...], sc_ref[0], sh_ref[0]).astype(BF16)
        proj = jnp.dot(h, win_ref[...], preferred_element_type=F32)
        u = proj[:, SC_WIDTH:2 * SC_WIDTH] * proj[:, 2 * SC_WIDTH:]
        y = (proj[:, :SC_WIDTH] * _causal_conv(u, u_tail, cw_ref)).astype(BF16)
        u_tail = u[SC_SUB_TILE - SUBLANES:, :]
        o_ref[r0:r0 + SC_SUB_TILE, :] = x + gate_ref[0] * jnp.dot(
            y, wout_ref[...], preferred_element_type=F32)
    tail_ref[...] = u_tail


def _mixer1(x, g, sc, sh, gate, w_in, conv_w, w_out, tm):
    per_batch = SEQ // tm
    return pl.pallas_call(
        functools.partial(_mixer1_kernel, per_batch=per_batch),
        grid=(TOKENS // tm,),
        in_specs=[
            pl.BlockSpec((tm, D_MODEL), lambda i: (i, 0)),
            pl.BlockSpec((1, D_MODEL), lambda i: (0, 0)),
            pl.BlockSpec((1, 1, D_MODEL), lambda i: (i // per_batch, 0, 0)),
            pl.BlockSpec((1, 1, D_MODEL), lambda i: (i // per_batch, 0, 0)),
            pl.BlockSpec((1, 1, D_MODEL), lambda i: (i // per_batch, 0, 0)),
            pl.BlockSpec((D_MODEL, 3 * SC_WIDTH), lambda i: (0, 0)),
            pl.BlockSpec((SC_CONV_K, SC_WIDTH), lambda i: (0, 0)),
            pl.BlockSpec((SC_WIDTH, D_MODEL), lambda i: (0, 0)),
        ],
        out_specs=pl.BlockSpec((tm, D_MODEL), lambda i: (i, 0)),
        out_shape=jax.ShapeDtypeStruct((TOKENS, D_MODEL), F32),
        scratch_shapes=[pltpu.VMEM((SUBLANES, SC_WIDTH), F32)],
        compiler_params=pltpu.CompilerParams(
            dimension_semantics=("arbitrary",), vmem_limit_bytes=VMEM_LIMIT),
        name="sc_mixer",
    )(x, g, sc, sh, gate, w_in, conv_w, w_out)


def _mlp_kernel(x_ref, g_ref, sc_ref, sh_ref, gate_ref, up_ref, down_ref, fw_ref, *rest, final_norm):
    n_casts = len(rest) // 2
    o_ref = rest[n_casts]
    for src, dst in zip(rest[:n_casts], rest[n_casts + 1:]):
        dst[...] = src[...].astype(BF16)
    h = _norm_mod(x_ref[...], g_ref[...], sc_ref[0], sh_ref[0]).astype(BF16)
    acc = None
    for c0 in range(0, D_FF, MLP_FF_CHUNK):
        a = jnp.maximum(jnp.dot(h, up_ref[:, c0:c0 + MLP_FF_CHUNK], preferred_element_type=F32), 0.0)
        part = jnp.dot((a * a).astype(BF16), down_ref[c0:c0 + MLP_FF_CHUNK, :],
                       preferred_element_type=F32)
        acc = part if acc is None else acc + part
    out = x_ref[...] + gate_ref[0] * acc
    if final_norm:
        ms = jnp.mean(out * out, axis=-1, keepdims=True)
        out = out * lax.rsqrt(ms + NORM_EPS) * fw_ref[...]
    o_ref[...] = out


def _mlp(x, g, sc, sh, gate, up, down, final_w, tm, final_norm, casts=()):
    per_batch = SEQ // tm
    steps = TOKENS // tm
    cast_in_specs, cast_out_specs, cast_shapes = [], [], []
    for w, layer in casts:
        _, rows, cols = w.shape
        cast_in_specs.append(pl.BlockSpec((None, rows // steps, cols), lambda i, layer=layer: (layer, i, 0)))
        cast_out_specs.append(pl.BlockSpec((rows // steps, cols), lambda i: (i, 0)))
        cast_shapes.append(jax.ShapeDtypeStruct((rows, cols), BF16))
    return pl.pallas_call(
        functools.partial(_mlp_kernel, final_norm=final_norm),
        grid=(steps,),
        in_specs=[
            pl.BlockSpec((tm, D_MODEL), lambda i: (i, 0)),
            pl.BlockSpec((1, D_MODEL), lambda i: (0, 0)),
            pl.BlockSpec((1, 1, D_MODEL), lambda i: (i // per_batch, 0, 0)),
            pl.BlockSpec((1, 1, D_MODEL), lambda i: (i // per_batch, 0, 0)),
            pl.BlockSpec((1, 1, D_MODEL), lambda i: (i // per_batch, 0, 0)),
            pl.BlockSpec((D_MODEL, D_FF), lambda i: (0, 0), pipeline_mode=pl.Buffered(1)),
            pl.BlockSpec((D_FF, D_MODEL), lambda i: (0, 0), pipeline_mode=pl.Buffered(1)),
            pl.BlockSpec((1, D_MODEL), lambda i: (0, 0)),
        ] + cast_in_specs,
        out_specs=[pl.BlockSpec((tm, D_MODEL), lambda i: (i, 0))] + cast_out_specs,
        out_shape=[jax.ShapeDtypeStruct((TOKENS, D_MODEL), F32)] + cast_shapes,
        compiler_params=pltpu.CompilerParams(
            dimension_semantics=("arbitrary",), vmem_limit_bytes=VMEM_LIMIT_FUSED),
        name="mlp_final" if final_norm else "mlp",
    )(x, g, sc, sh, gate, up, down, final_w, *[w for w, _ in casts])


def kernel(x, c, ada_w, ada_b, mix_norm_w, mlp_norm_w, mlp_up, mlp_down, ssd_in_w, ssd_conv_w,
           ssd_conv_b, ssd_dt_bias, ssd_A_log, ssd_D, ssd_norm_w, ssd_out_w, sc_in_w, sc_conv_w,
           sc_out_w, final_norm_w):
    xf = x.reshape(TOKENS, D_MODEL)
    mod, w_out = _ada_mod(c, ada_w, ada_b, ssd_out_w)
    mod = mod.reshape(DEPTH, BATCH, ADA_MODS, 1, D_MODEL)

    def mods(i):
        return [mod[i, :, k] for k in range(ADA_MODS)]

    sh_m, sc_m, g_m, sh_f, sc_f, g_f = mods(0)
    w_zx = ssd_in_w[0].astype(BF16)
    w_dt = jnp.zeros((D_MODEL, LANES), BF16).at[:, :SSD_N_HEADS].set(
        ssd_in_w[0, :, SSD_ZX_DIM:].astype(BF16))
    dt_bias = jnp.zeros((1, LANES), F32).at[0, :SSD_N_HEADS].set(ssd_dt_bias[0])
    xf, up0, down0 = _mixer0(xf, mix_norm_w[0].reshape(1, D_MODEL), sc_m, sh_m, g_m, w_zx, w_dt, dt_bias,
                             w_out, ssd_conv_w[0], ssd_conv_b[0], ssd_A_log[0],
                             ssd_D[0], ssd_norm_w[0], mlp_up, mlp_down)
    xf, up1, down1, sc_in_bf, sc_out_bf = _mlp(
        xf, mlp_norm_w[0].reshape(1, D_MODEL), sc_f, sh_f, g_f, up0, down0,
        final_norm_w.reshape(1, D_MODEL), tm=1024, final_norm=False,
        casts=[(mlp_up, 1), (mlp_down, 1), (sc_in_w, 0), (sc_out_w, 0)])

    sh_m, sc_m, g_m, sh_f, sc_f, g_f = mods(1)
    xf = _mixer1(xf, mix_norm_w[1].reshape(1, D_MODEL), sc_m, sh_m, g_m, sc_in_bf, sc_conv_w[0],
                 sc_out_bf, tm=1024)
    xf, = _mlp(xf, mlp_norm_w[1].reshape(1, D_MODEL), sc_f, sh_f, g_f, up1, down1,
               final_norm_w.reshape(1, D_MODEL), tm=1024, final_norm=True)
    return xf.reshape(BATCH, SEQ, D_MODEL)
```
